```python
import math
import jax, jax.numpy as jnp
from jax import lax
import numpy as np

D_MODEL = 1024
BATCH = 8
SEQ = 4096
DEPTH = 1

GRID_W = 64
CTX_LEN = 256
N_MOD = 9
D_FF = 2816
EPS = 1e-6
SSD_HEADS = 8
SSD_HEAD_DIM = 64
SSD_WIDTH = SSD_HEADS * SSD_HEAD_DIM
SSD_GROUPS = 2
SSD_STATE = 128
SSD_CONV = 5
SSD_CHUNK = 128
N_DIR = 2
CONV_CH = SSD_WIDTH + 2 * SSD_GROUPS * SSD_STATE
MLA_HEADS = 8
QK_NOPE = 64
QK_ROPE = 32
V_DIM = 64
Q_LORA = 768
KV_LORA = 256
MLA_WIDTH = MLA_HEADS * V_DIM
MLA_SCALE = (QK_NOPE + QK_ROPE) ** -0.5
ROPE_BASE = 10000.0
Q_BLOCK = 128
MIX_WIDTH = SSD_WIDTH + MLA_WIDTH
SPLIT_Z = SSD_WIDTH
SPLIT_XBC = SPLIT_Z + CONV_CH
SPLIT_DT = SPLIT_XBC + N_DIR * SSD_HEADS
SPLIT_CQ = SPLIT_DT + Q_LORA
SPLIT_CKV = SPLIT_CQ + KV_LORA
IN_DIM = SPLIT_CKV + QK_ROPE
IN_SPLITS = (SPLIT_Z, SPLIT_XBC, SPLIT_DT, SPLIT_CQ, SPLIT_CKV)

kernel_name = 'hymba_ssd_mla_macaron_dit'

F32 = jnp.float32


def rmsnorm(x, g):
    xf = x.astype(F32)
    y = xf * lax.rsqrt(jnp.mean(xf * xf, axis=-1, keepdims=True) + EPS)
    return y.astype(x.dtype) * g


def modulate(h, shift, scale):
    return h * (1 + scale) + shift


def swiglu(h, w_in, w_out):
    gate, up = jnp.split(h @ w_in, 2, axis=-1)
    return (jax.nn.silu(gate) * up) @ w_out


def ffn_half_step(h, m, k, norm_g, w_in, w_out):
    hn = modulate(rmsnorm(h, norm_g), m[:, :, k], m[:, :, k + 1])
    return h + 0.5 * m[:, :, k + 2] * swiglu(hn, w_in, w_out)


def _flip(t):
    return jnp.flip(t, axis=1)


def dwconv_centred(u, w, b):
    y = lax.conv_general_dilated(u, w[:, None, :], window_strides=(1,),
                                 padding=((SSD_CONV // 2, SSD_CONV // 2),),
                                 dimension_numbers=('NWC', 'WIO', 'NWC'),
                                 feature_group_count=u.shape[-1])
    return y + b


def ssd_inputs(xbc, dt_raw, conv_w, conv_b, dt_bias):
    b, L, _ = xbc.shape
    xbc = jax.nn.silu(dwconv_centred(xbc, conv_w, conv_b))
    xs, bm, cm = jnp.split(xbc, [SSD_WIDTH, SSD_WIDTH + SSD_GROUPS * SSD_STATE], axis=-1)
    xh = xs.reshape(b, L, SSD_HEADS, SSD_HEAD_DIM)
    bm = bm.reshape(b, L, SSD_GROUPS, SSD_STATE)
    cm = cm.reshape(b, L, SSD_GROUPS, SSD_STATE)
    dt = jax.nn.softplus(dt_raw.astype(F32).reshape(b, L, N_DIR, SSD_HEADS) + dt_bias.astype(F32))
    return xh, bm, cm, dt


def ssd_scan(xh, dt, a_neg, bm, cm, h0):
    b, L, H, P = xh.shape
    G, N = bm.shape[-2:]
    HG = H // G
    Q = SSD_CHUNK
    nc = L // Q
    xdt = (xh.astype(F32) * dt[..., None]).reshape(b, nc, Q, G, HG, P)
    a = (dt * a_neg).reshape(b, nc, Q, G, HG)
    a_cum = jnp.cumsum(jnp.moveaxis(a, 2, -1), axis=-1)
    bc = bm.astype(F32).reshape(b, nc, Q, G, N)
    cc = cm.astype(F32).reshape(b, nc, Q, G, N)
    seg = a_cum[..., :, None] - a_cum[..., None, :]
    lower = jnp.tril(jnp.ones((Q, Q), dtype=bool))
    decay_in = jnp.exp(jnp.where(lower, seg, -jnp.inf))
    cb = jnp.einsum('bcign,bcjgn->bcgij', cc, bc)
    y_diag = jnp.einsum('bcghij,bcjghp->bcighp', cb[:, :, :, None] * decay_in, xdt)
    decay_end = jnp.moveaxis(jnp.exp(a_cum[..., -1:] - a_cum), -1, 2)
    states = jnp.einsum('bcjgn,bcjghp->bcghpn', bc, xdt * decay_end[..., None])
    chunk_decay = jnp.exp(a_cum[..., -1])

    def step(h, inp):
        dec, st = inp
        return dec[..., None, None] * h + st, h

    h_last, h_enter = lax.scan(step, h0.reshape(b, G, HG, P, N),
                               (jnp.moveaxis(chunk_decay, 1, 0), jnp.moveaxis(states, 1, 0)))
    h_enter = jnp.moveaxis(h_enter, 0, 1)
    decay_from_start = jnp.moveaxis(jnp.exp(a_cum), -1, 2)
    y_off = jnp.einsum('bcign,bcghpn->bcighp', cc, h_enter) * decay_from_start[..., None]
    y = (y_diag + y_off).reshape(b, L, H, P)
    return y, h_last.reshape(b, H, P, N)


def ssd_merge(y_fwd, y_bwd, xh, z, d_skip, ssd_norm):
    b, L = xh.shape[:2]
    y = y_fwd + y_bwd + d_skip[:, None] * xh
    return rmsnorm(y.reshape(b, L, SSD_WIDTH) * jax.nn.silu(z), ssd_norm)


def axial_rope(t, row, col):
    half = QK_ROPE // 2
    inv = ROPE_BASE ** (-jnp.arange(0, half, 2, dtype=F32) / half)

    def rot(u, pos):
        ang = pos.astype(F32)[:, None] * inv
        cos = jnp.cos(ang)[:, None, :].astype(u.dtype)
        sin = jnp.sin(ang)[:, None, :].astype(u.dtype)
        u1, u2 = jnp.split(u, 2, axis=-1)
        return jnp.concatenate([u1 * cos - u2 * sin, u1 * sin + u2 * cos], axis=-1)

    t_row, t_col = jnp.split(t, 2, axis=-1)
    return jnp.concatenate([rot(t_row, row), rot(t_col, col)], axis=-1)


def mla_project(c_q, c_kv, k_pe, q_norm, w_uq, kv_norm, w_ukv, row, col):
    b, L, _ = c_q.shape
    q = (rmsnorm(c_q, q_norm) @ w_uq).reshape(b, L, MLA_HEADS, QK_NOPE + QK_ROPE)
    kv = (rmsnorm(c_kv, kv_norm) @ w_ukv).reshape(b, L, MLA_HEADS, QK_NOPE + V_DIM)
    q_nope, q_pe = jnp.split(q, [QK_NOPE], axis=-1)
    k_nope, v = jnp.split(kv, [QK_NOPE], axis=-1)
    k_pe = k_pe[:, :, None, :]
    if row is not None:
        q_pe = axial_rope(q_pe, row, col)
        k_pe = axial_rope(k_pe, row, col)
    q = jnp.concatenate([q_nope, q_pe], axis=-1)
    k = jnp.concatenate([k_nope, jnp.broadcast_to(k_pe, (b, L, MLA_HEADS, QK_ROPE))], axis=-1)
    return q, k, v


def softmax_attend(q, k, v):
    s = jnp.einsum('bqhd,bkhd->bhqk', q, k).astype(F32) * MLA_SCALE
    p = jax.nn.softmax(s, axis=-1).astype(v.dtype)
    return jnp.einsum('bhqk,bkhd->bqhd', p, v)


def hybrid_mixer(hx, hc, w_in, conv_w, conv_b, dt_bias, a_log, d_skip, ssd_norm,
                 q_norm, w_uq, kv_norm, w_ukv, w_out, need_ctx):
    b, L, _ = hx.shape
    cl = hc.shape[1]
    rows = L // GRID_W
    row = jnp.repeat(jnp.arange(rows, dtype=jnp.int32), GRID_W)
    col = jnp.arange(rows * GRID_W, dtype=jnp.int32) % GRID_W

    z_x, xbc_x, dt_x, cq_x, ckv_x, kpe_x = jnp.split(hx @ w_in, IN_SPLITS, axis=-1)
    z_c, xbc_c, dt_c, cq_c, ckv_c, kpe_c = jnp.split(hc @ w_in, IN_SPLITS, axis=-1)

    xh_x, bm_x, cm_x, dtv_x = ssd_inputs(xbc_x, dt_x, conv_w, conv_b, dt_bias)
    xh_c, bm_c, cm_c, dtv_c = ssd_inputs(xbc_c, dt_c, conv_w, conv_b, dt_bias)
    a_neg = -jnp.exp(a_log.astype(F32))
    h0 = jnp.zeros((b, SSD_HEADS, SSD_HEAD_DIM, SSD_STATE), F32)
    yf_c, hf = ssd_scan(xh_c, dtv_c[:, :, 0], a_neg[0], bm_c, cm_c, h0)
    yf_x, _ = ssd_scan(xh_x, dtv_x[:, :, 0], a_neg[0], bm_x, cm_x, hf)
    yb_c, hb = ssd_scan(_flip(xh_c), _flip(dtv_c[:, :, 1]), a_neg[1], _flip(bm_c), _flip(cm_c), h0)
    yb_x, _ = ssd_scan(_flip(xh_x), _flip(dtv_x[:, :, 1]), a_neg[1], _flip(bm_x), _flip(cm_x), hb)
    ssd_x = ssd_merge(yf_x, _flip(yb_x), xh_x, z_x, d_skip, ssd_norm)

    q_x, k_x, v_x = mla_project(cq_x, ckv_x, kpe_x, q_norm, w_uq, kv_norm, w_ukv, row, col)
    q_c, k_c, v_c = mla_project(cq_c, ckv_c, kpe_c, q_norm, w_uq, kv_norm, w_ukv, None, None)
    k_all = jnp.concatenate([k_c, k_x], axis=1)
    v_all = jnp.concatenate([v_c, v_x], axis=1)
    nb = L // Q_BLOCK
    q_blocks = jnp.swapaxes(q_x.reshape(b, nb, Q_BLOCK, MLA_HEADS, QK_NOPE + QK_ROPE), 0, 1)
    o_blocks = lax.map(lambda qb: softmax_attend(qb, k_all, v_all), q_blocks)
    mla_x = jnp.swapaxes(o_blocks, 0, 1).reshape(b, L, MLA_WIDTH)

    y_x = jnp.concatenate([ssd_x, mla_x], axis=-1) @ w_out
    if need_ctx:
        ssd_c = ssd_merge(yf_c, _flip(yb_c), xh_c, z_c, d_skip, ssd_norm)
        mla_c = softmax_attend(q_c, k_c, v_c).reshape(b, cl, MLA_WIDTH)
        y_c = jnp.concatenate([ssd_c, mla_c], axis=-1) @ w_out
        return y_x, y_c
    return y_x, None


def setup_inputs(seed: int = 0) -> dict:
    key = jax.random.key(seed)
    ks = jax.random.split(key, 28)

    def nrm(k, shape, scale):
        return jax.random.normal(k, shape, F32) * scale

    def gain(k, shape):
        return 1.0 + 0.02 * jax.random.normal(k, shape, F32)

    dt0 = jnp.exp(jax.random.uniform(ks[11], (DEPTH, N_DIR, SSD_HEADS), F32,
                                     math.log(1e-3), math.log(1e-1)))
    in_col_scale = jnp.ones((IN_DIM,), F32).at[SPLIT_XBC:SPLIT_DT].set(0.1)
    return {
        'x': nrm(ks[0], (BATCH, SEQ, D_MODEL), 1.0),
        'c': nrm(ks[1], (BATCH, D_MODEL), 1.0),
        'ctx': nrm(ks[2], (BATCH, CTX_LEN, D_MODEL), 1.0),
        'c_ctx': nrm(ks[3], (D_MODEL,), 1.0),
        'w_ada': nrm(ks[4], (DEPTH, D_MODEL, N_MOD * D_MODEL), 0.5 * D_MODEL ** -0.5),
        'b_ada': nrm(ks[5], (DEPTH, N_MOD * D_MODEL), 0.02),
        'norm_ffn1': gain(ks[6], (DEPTH, D_MODEL)),
        'w_ffn1_in': nrm(ks[7], (DEPTH, D_MODEL, 2 * D_FF), D_MODEL ** -0.5),
        'w_ffn1_out': nrm(ks[8], (DEPTH, D_FF, D_MODEL), D_FF ** -0.5),
        'norm_mix': gain(ks[9], (DEPTH, D_MODEL)),
        'w_in': nrm(ks[10], (DEPTH, D_MODEL, IN_DIM), D_MODEL ** -0.5) * in_col_scale,
        'conv_w': nrm(ks[12], (DEPTH, SSD_CONV, CONV_CH), SSD_CONV ** -0.5),
        'conv_b': nrm(ks[13], (DEPTH, CONV_CH), 0.02),
        'dt_bias': dt0 + jnp.log(-jnp.expm1(-dt0)),
        'a_log': jnp.log(jax.random.uniform(ks[14], (DEPTH, N_DIR, SSD_HEADS), F32, 1.0, 16.0)),
        'd_skip': gain(ks[15], (DEPTH, SSD_HEADS)),
        'ssd_norm': gain(ks[16], (DEPTH, SSD_WIDTH)),
        'q_norm': gain(ks[17], (DEPTH, Q_LORA)),
        'w_uq': nrm(ks[18], (DEPTH, Q_LORA, MLA_HEADS * (QK_NOPE + QK_ROPE)), Q_LORA ** -0.5),
        'kv_norm': gain(ks[19], (DEPTH, KV_LORA)),
        'w_ukv': nrm(ks[20], (DEPTH, KV_LORA, MLA_HEADS * (QK_NOPE + V_DIM)), KV_LORA ** -0.5),
        'w_out': nrm(ks[21], (DEPTH, MIX_WIDTH, D_MODEL), MIX_WIDTH ** -0.5),
        'norm_ffn2': gain(ks[22], (DEPTH, D_MODEL)),
        'w_ffn2_in': nrm(ks[23], (DEPTH, D_MODEL, 2 * D_FF), D_MODEL ** -0.5),
        'w_ffn2_out': nrm(ks[24], (DEPTH, D_FF, D_MODEL), D_FF ** -0.5),
        'final_norm': gain(ks[25], (D_MODEL,)),
    }


def reference(x, c, ctx, c_ctx, w_ada, b_ada, norm_ffn1, w_ffn1_in, w_ffn1_out, norm_mix,
              w_in, conv_w, conv_b, dt_bias, a_log, d_skip, ssd_norm, q_norm, w_uq,
              kv_norm, w_ukv, w_out, norm_ffn2, w_ffn2_in, w_ffn2_out, final_norm):
    b = x.shape[0]
    for i in range(DEPTH):
        last = i == DEPTH - 1
        m_x = (jax.nn.silu(c) @ w_ada[i] + b_ada[i]).reshape(b, 1, N_MOD, D_MODEL)
        m_c = (jax.nn.silu(c_ctx) @ w_ada[i] + b_ada[i]).reshape(1, 1, N_MOD, D_MODEL)
        x = ffn_half_step(x, m_x, 0, norm_ffn1[i], w_ffn1_in[i], w_ffn1_out[i])
        ctx = ffn_half_step(ctx, m_c, 0, norm_ffn1[i], w_ffn1_in[i], w_ffn1_out[i])
        hx = modulate(rmsnorm(x, norm_mix[i]), m_x[:, :, 3], m_x[:, :, 4])
        hc = modulate(rmsnorm(ctx, norm_mix[i]), m_c[:, :, 3], m_c[:, :, 4])
        y_x, y_c = hybrid_mixer(hx, hc, w_in[i], conv_w[i], conv_b[i], dt_bias[i], a_log[i],
                                d_skip[i], ssd_norm[i], q_norm[i], w_uq[i], kv_norm[i],
                                w_ukv[i], w_out[i], not last)
        x = x + m_x[:, :, 5] * y_x
        x = ffn_half_step(x, m_x, 6, norm_ffn2[i], w_ffn2_in[i], w_ffn2_out[i])
        if not last:
            ctx = ctx + m_c[:, :, 5] * y_c
            ctx = ffn_half_step(ctx, m_c, 6, norm_ffn2[i], w_ffn2_in[i], w_ffn2_out[i])
    return rmsnorm(x, final_norm)
```

```python
import functools
import math

import jax
import jax.numpy as jnp
import numpy as np
from jax import lax
from jax.experimental import pallas as pl
from jax.experimental.pallas import tpu as pltpu

F32 = jnp.float32
BF16 = jnp.bfloat16

D_MODEL = 1024
GRID_W = 64
N_MOD = 9
D_FF = 2816
EPS = 1e-6
SSD_HEADS = 8
SSD_HEAD_DIM = 64
SSD_WIDTH = SSD_HEADS * SSD_HEAD_DIM
SSD_GROUPS = 2
SSD_STATE = 128
SSD_CONV = 5
SSD_CHUNK = 128
N_DIR = 2
CONV_CH = SSD_WIDTH + 2 * SSD_GROUPS * SSD_STATE
MLA_HEADS = 8
QK_NOPE = 64
QK_ROPE = 32
V_DIM = 64
Q_LORA = 768
KV_LORA = 256
MLA_SCALE = (QK_NOPE + QK_ROPE) ** -0.5
ROPE_BASE = 10000.0
SPLIT_Z = SSD_WIDTH
SPLIT_XBC = SPLIT_Z + CONV_CH
SPLIT_DT = SPLIT_XBC + N_DIR * SSD_HEADS
SPLIT_CQ = SPLIT_DT + Q_LORA
SPLIT_CKV = SPLIT_CQ + KV_LORA

LANES = 128
SUBLANES = 8
HEAD_PAD = 128
MLA_PAD = MLA_HEADS * HEAD_PAD
FF_CHUNK = 256
SMALL_W = LANES
DT_LANE0 = 2 * QK_ROPE
W_ALL = SSD_WIDTH + CONV_CH + Q_LORA + KV_LORA + SMALL_W
VMEM_LIMIT = 56 * 1024 * 1024


def _dot(a, b):
    return jnp.dot(a, b, preferred_element_type=F32)


def _dot_nt(a, b):
    return lax.dot_general(a, b, (((1,), (1,)), ((), ())), preferred_element_type=F32)


def _dot_tn(a, b):
    return lax.dot_general(a, b, (((0,), (0,)), ((), ())), preferred_element_type=F32)


def _silu(x):
    return x * (1.0 / (1.0 + jnp.exp(-x)))


def _rms(x, g):
    return x * lax.rsqrt(jnp.mean(x * x, axis=-1, keepdims=True) + EPS) * g


def _const_spec(shape):
    zeros = (0,) * len(shape)
    return pl.BlockSpec(shape, lambda *_: zeros, pipeline_mode=pl.Buffered(1))


def _params(n_axes):
    return pltpu.CompilerParams(dimension_semantics=("arbitrary",) * n_axes,
                                vmem_limit_bytes=VMEM_LIMIT)


def _ada_kernel(c_ref, w_ref, b_ref, o_ref):
    s = _silu(c_ref[...]).astype(BF16)
    o_ref[...] = _dot(s, w_ref[...].astype(BF16)) + b_ref[...]


def _ada(c_rows, w, b):
    rows, d = c_rows.shape
    n = w.shape[1]
    tn = 1024
    return pl.pallas_call(
        _ada_kernel,
        out_shape=jax.ShapeDtypeStruct((rows, n), F32),
        grid=(n // tn,),
        in_specs=[pl.BlockSpec((rows, d), lambda j: (0, 0)),
                  pl.BlockSpec((d, tn), lambda j: (0, j)),
                  pl.BlockSpec((1, tn), lambda j: (0, j))],
        out_specs=pl.BlockSpec((rows, tn), lambda j: (0, j)),
        compiler_params=_params(1),
        name="ada",
    )(c_rows, w, b)


def _ffn_kernel(x_ref, m_ref, g_ref, wi_ref, wo_ref, *rest, k, final):
    if final:
        fg_ref, o_ref, acc_ref = rest
    else:
        o_ref, acc_ref = rest
    x = x_ref[...]
    m = m_ref[0]
    shift, scale, gate = m[k:k + 1], m[k + 1:k + 2], m[k + 2:k + 3]
    hn = (_rms(x, g_ref[...]) * (1 + scale) + shift).astype(BF16)
    for j in range(D_FF // FF_CHUNK):
        lo = j * FF_CHUNK
        g = _dot(hn, wi_ref[:, lo:lo + FF_CHUNK])
        u = _dot(hn, wi_ref[:, D_FF + lo:D_FF + lo + FF_CHUNK])
        a = (_silu(g) * u).astype(BF16)
        part = _dot(a, wo_ref[lo:lo + FF_CHUNK, :])
        if j == 0:
            acc_ref[...] = part
        else:
            acc_ref[...] += part
    y = x + 0.5 * gate * acc_ref[...]
    if final:
        y = _rms(y, fg_ref[...])
    o_ref[...] = y


def _ffn(x2d, m_all, m_index, g, wi, wo, k, tm, final_g=None):
    n, d = x2d.shape
    final = final_g is not None
    in_specs = [pl.BlockSpec((tm, d), lambda i: (i, 0)),
                pl.BlockSpec((1, N_MOD, d), lambda i: (m_index(i), 0, 0)),
                _const_spec((1, d)),
                _const_spec(wi.shape),
                _const_spec(wo.shape)]
    args = [x2d, m_all, g, wi, wo]
    if final:
        in_specs.append(_const_spec((1, d)))
        args.append(final_g)
    return pl.pallas_call(
        functools.partial(_ffn_kernel, k=k, final=final),
        out_shape=jax.ShapeDtypeStruct((n, d), F32),
        grid=(n // tm,),
        in_specs=in_specs,
        out_specs=pl.BlockSpec((tm, d), lambda i: (i, 0)),
        scratch_shapes=[pltpu.VMEM((tm, d), F32)],
        compiler_params=_params(1),
        name="ffn_final" if final else "ffn",
    )(*args)


def _in_proj_kernel(x_ref, m_ref, g_ref, w_ref, kvn_ref, wuk_ref, wuv_ref, e_ref, tk_ref,
                    *rest, with_q, n_alias):
    if with_q:
        qn_ref, wqa_ref, wqb_ref, tqc_ref, tqs_ref = rest[:5]
        rest = rest[5:]
    rest = rest[n_alias:]
    xbc_ref, small_ref, k_ref, v_ref = rest[:4]
    if with_q:
        z_ref, q_ref = rest[4:6]
    m = m_ref[0]
    hn = (_rms(x_ref[0], g_ref[...]) * (1 + m[4:5]) + m[3:4]).astype(BF16)
    o_xbc = SSD_WIDTH
    o_cq = o_xbc + CONV_CH
    o_ckv = o_cq + Q_LORA
    o_small = o_ckv + KV_LORA
    xbc_ref[0] = _dot(hn, w_ref[:, o_xbc:o_cq])
    small = _dot(hn, w_ref[:, o_small:o_small + SMALL_W])
    small_ref[0] = small
    ckv = _rms(_dot(hn, w_ref[:, o_ckv:o_small]), kvn_ref[...]).astype(BF16)
    t = small * tk_ref[...]
    kpe = (t + pltpu.roll(t, LANES - QK_ROPE, axis=1)).astype(BF16)
    k_ref[0] = (_dot(ckv, wuk_ref[...]) + _dot(kpe, e_ref[...])).astype(BF16)
    v_ref[0] = _dot(ckv, wuv_ref[...]).astype(BF16)
    if with_q:
        z_ref[0] = _dot(hn, w_ref[:, 0:o_xbc])
        cq = _rms(_dot(hn, w_ref[:, o_cq:o_ckv]), qn_ref[...]).astype(BF16)
        tqc = jnp.concatenate([tqc_ref[...]] * MLA_HEADS, axis=1)
        tqs = jnp.concatenate([tqs_ref[...]] * MLA_HEADS, axis=1)
        q_ref[0] = (_dot(cq, wqa_ref[...]) * tqc + _dot(cq, wqb_ref[...]) * tqs).astype(BF16)


def _in_proj(h3d, m_all, m_index, g, w_all, kvn, wuk, wuv, e_mat, tk, tm, row0, s_total,
             q_args=None, alias_bufs=()):
    b, rows, d = h3d.shape
    with_q = q_args is not None
    blk0 = row0 // tm
    tok = lambda c: pl.BlockSpec((1, tm, c), lambda bi, i: (bi, i, 0))
    comb = lambda c: pl.BlockSpec((1, tm, c), lambda bi, i: (bi, blk0 + i, 0))
    in_specs = [tok(d),
                pl.BlockSpec((1, N_MOD, d), lambda bi, i: (m_index(bi), 0, 0)),
                _const_spec(g.shape), _const_spec(w_all.shape), _const_spec(kvn.shape),
                _const_spec(wuk.shape), _const_spec(wuv.shape), _const_spec(e_mat.shape),
                pl.BlockSpec((tm, LANES), lambda bi, i: (i, 0))]
    args = [h3d, m_all, g, w_all, kvn, wuk, wuv, e_mat, tk]
    out_shape = [jax.ShapeDtypeStruct((b, s_total, CONV_CH), F32),
                 jax.ShapeDtypeStruct((b, s_total, SMALL_W), F32),
                 jax.ShapeDtypeStruct((b, s_total, MLA_PAD), BF16),
                 jax.ShapeDtypeStruct((b, s_total, MLA_PAD), BF16)]
    out_specs = [comb(CONV_CH), comb(SMALL_W), comb(MLA_PAD), comb(MLA_PAD)]
    if with_q:
        qn, wqa, wqb, tqc, tqs = q_args
        in_specs += [_const_spec(qn.shape), _const_spec(wqa.shape), _const_spec(wqb.shape),
                     pl.BlockSpec((tm, LANES), lambda bi, i: (i, 0)),
                     pl.BlockSpec((tm, LANES), lambda bi, i: (i, 0))]
        args += [qn, wqa, wqb, tqc, tqs]
        out_shape += [jax.ShapeDtypeStruct((b, rows, SSD_WIDTH), F32),
                      jax.ShapeDtypeStruct((b, rows, MLA_PAD), BF16)]
        out_specs += [tok(SSD_WIDTH), tok(MLA_PAD)]
    aliases = {len(args) + j: j for j in range(len(alias_bufs))}
    in_specs += [pl.BlockSpec(memory_space=pl.ANY)] * len(alias_bufs)
    args += list(alias_bufs)
    return pl.pallas_call(
        functools.partial(_in_proj_kernel, with_q=with_q, n_alias=len(alias_bufs)),
        out_shape=out_shape,
        grid=(b, rows // tm),
        in_specs=in_specs,
        out_specs=out_specs,
        input_output_aliases=aliases,
        compiler_params=_params(2),
        name="in_proj_q" if with_q else "in_proj_ctx",
    )(*args)


def _tile_rows(blk, reps):
    return jnp.concatenate([blk] * reps, axis=0)


def _ssd_kernel(cur_ref, prev_ref, next_ref, small_ref, cw_ref, cb_ref, dtb_ref, aneg_ref,
                dskip_ref, y_ref, h_ref, *, direction, ncx, ncc):
    q = SSD_CHUNK
    t = pl.program_id(1)
    c = jnp.where(t < ncc, ncx + t, t - ncc) if direction == 0 else ncx + ncc - 1 - t

    @pl.when(t == 0)
    def _():
        h_ref[...] = jnp.zeros_like(h_ref)

    has_prev = jnp.logical_and(c != 0, c != ncx).astype(F32)
    has_next = jnp.logical_and(c != ncx - 1, c != ncx + ncc - 1).astype(F32)
    cur = cur_ref[0]
    prev = prev_ref[0] * has_prev
    nxt = next_ref[0] * has_next
    reps = q // SUBLANES
    row = lax.broadcasted_iota(jnp.int32, (q, CONV_CH), 0)
    xm2 = jnp.where(row < 2, _tile_rows(pltpu.roll(prev, 2, axis=0), reps), pltpu.roll(cur, 2, axis=0))
    xm1 = jnp.where(row < 1, _tile_rows(pltpu.roll(prev, 1, axis=0), reps), pltpu.roll(cur, 1, axis=0))
    xp1 = jnp.where(row >= q - 1, _tile_rows(pltpu.roll(nxt, SUBLANES - 1, axis=0), reps),
                    pltpu.roll(cur, q - 1, axis=0))
    xp2 = jnp.where(row >= q - 2, _tile_rows(pltpu.roll(nxt, SUBLANES - 2, axis=0), reps),
                    pltpu.roll(cur, q - 2, axis=0))
    cw = cw_ref[...]
    u = (xm2 * cw[0:1] + xm1 * cw[1:2] + cur * cw[2:3] + xp1 * cw[3:4] + xp2 * cw[4:5]) + cb_ref[...]
    u = _silu(u)
    xs = u[:, 0:SSD_WIDTH]
    o_b = SSD_WIDTH
    o_c = SSD_WIDTH + SSD_GROUPS * SSD_STATE
    bm = u[:, o_b:o_c].astype(BF16)
    cm = u[:, o_c:CONV_CH].astype(BF16)

    pre = small_ref[0] + dtb_ref[...]
    dtv = jnp.maximum(pre, 0.0) + jnp.log(1.0 + jnp.exp(-jnp.abs(pre)))
    a = dtv * aneg_ref[...]
    ri = lax.broadcasted_iota(jnp.int32, (q, q), 0)
    ci = lax.broadcasted_iota(jnp.int32, (q, q), 1)
    keep = (ri >= ci) if direction == 0 else (ri <= ci)
    tri = jnp.where(keep, 1.0, 0.0).astype(BF16)
    a1 = a.astype(BF16)
    r1 = a - a1.astype(F32)
    a2 = r1.astype(BF16)
    a3 = (r1 - a2.astype(F32)).astype(BF16)
    acum = _dot(tri, a1) + _dot(tri, a2) + _dot(tri, a3)
    acum_t = acum.T
    last = q - 1 if direction == 0 else 0
    dend = jnp.exp(acum[last:last + 1, :] - acum)
    dstart = jnp.exp(acum)

    lane = lax.broadcasted_iota(jnp.int32, (q, LANES), 1)
    head_lane = lambda h: DT_LANE0 + SSD_HEADS * direction + h

    def expand(arr):
        cols = []
        for j in range(SSD_HEADS // 2):
            la, lb = head_lane(2 * j), head_lane(2 * j + 1)
            cols.append(jnp.where(lane < SSD_HEAD_DIM, arr[:, la:la + 1], arr[:, lb:lb + 1]))
        return jnp.concatenate(cols, axis=1)

    dt_x = expand(dtv)
    dend_x = expand(dend)
    dstart_x = expand(dstart)
    xdt = xs * dt_x
    xend = (xdt * dend_x).astype(BF16)
    gw = SSD_WIDTH // SSD_GROUPS
    hpg = SSD_HEADS // SSD_GROUPS
    glane = lax.broadcasted_iota(jnp.int32, (q, gw), 1)
    ys = []
    for g in range(SSD_GROUPS):
        cg = cm[:, g * SSD_STATE:(g + 1) * SSD_STATE]
        bg = bm[:, g * SSD_STATE:(g + 1) * SSD_STATE]
        cbm = _dot_nt(cg, bg)
        xg = xdt[:, g * gw:(g + 1) * gw]
        yg = None
        for hh in range(hpg):
            hl = head_lane(g * hpg + hh)
            seg = acum[:, hl:hl + 1] - acum_t[hl:hl + 1, :]
            lmat = (cbm * jnp.exp(jnp.where(keep, seg, -jnp.inf))).astype(BF16)
            in_head = jnp.logical_and(glane >= hh * SSD_HEAD_DIM, glane < (hh + 1) * SSD_HEAD_DIM)
            part = _dot(lmat, jnp.where(in_head, xg, 0.0).astype(BF16))
            yg = part if yg is None else yg + part
        hst = h_ref[g]
        yoff = _dot(cg, hst.astype(BF16)) * dstart_x[:, g * gw:(g + 1) * gw]
        ys.append(yg + yoff)
        h_ref[g] = dstart_x[last:last + 1, g * gw:(g + 1) * gw] * hst + _dot_tn(bg, xend[:, g * gw:(g + 1) * gw])

    @pl.when(t >= ncc)
    def _():
        y = jnp.concatenate(ys, axis=1)
        if direction == 0:
            y = y + dskip_ref[...] * xs
        y_ref[0] = y


def _ssd(xbc_all, small_all, cw, cb, dtb, aneg, dskip, direction, l_x, l_c):
    b, s_total, _ = xbc_all.shape
    q = SSD_CHUNK
    ncx, ncc = l_x // q, l_c // q
    hb = q // SUBLANES
    n8 = s_total // SUBLANES

    def chunk(t):
        if direction == 0:
            return jnp.where(t < ncc, ncx + t, t - ncc)
        return ncx + ncc - 1 - t

    def xchunk(t):
        return chunk(jnp.maximum(t, ncc))

    return pl.pallas_call(
        functools.partial(_ssd_kernel, direction=direction, ncx=ncx, ncc=ncc),
        out_shape=jax.ShapeDtypeStruct((b, l_x, SSD_WIDTH), F32),
        grid=(b, ncx + ncc),
        in_specs=[pl.BlockSpec((1, q, CONV_CH), lambda bi, t: (bi, chunk(t), 0)),
                  pl.BlockSpec((1, SUBLANES, CONV_CH),
                               lambda bi, t: (bi, jnp.maximum(chunk(t) * hb - 1, 0), 0)),
                  pl.BlockSpec((1, SUBLANES, CONV_CH),
                               lambda bi, t: (bi, jnp.minimum(chunk(t) * hb + hb, n8 - 1), 0)),
                  pl.BlockSpec((1, q, SMALL_W), lambda bi, t: (bi, chunk(t), 0)),
                  _const_spec(cw.shape), _const_spec(cb.shape), _const_spec(dtb.shape),
                  _const_spec(aneg.shape), _const_spec(dskip.shape)],
        out_specs=pl.BlockSpec((1, q, SSD_WIDTH), lambda bi, t: (bi, xchunk(t), 0)),
        scratch_shapes=[pltpu.VMEM((SSD_GROUPS, SSD_STATE, SSD_WIDTH // SSD_GROUPS), F32)],
        compiler_params=_params(2),
        name="ssd_fwd" if direction == 0 else "ssd_bwd",
    )(xbc_all, xbc_all, xbc_all, small_all, cw, cb, dtb, aneg, dskip)


def _attn_kernel(q_ref, k_ref, v_ref, o_ref):
    s = _dot_nt(q_ref[0], k_ref[0])
    m = jnp.max(s, axis=1, keepdims=True)
    p = jnp.exp(s - m)
    l = jnp.sum(p, axis=1, keepdims=True)
    o = _dot(p.astype(BF16), v_ref[0])
    o_ref[0] = (o / l).astype(BF16)


def _attn(q, k_all, v_all, tq):
    b, l_x, _ = q.shape
    s_total = k_all.shape[1]
    return pl.pallas_call(
        _attn_kernel,
        out_shape=jax.ShapeDtypeStruct((b, l_x, MLA_PAD), BF16),
        grid=(b, MLA_HEADS, l_x // tq),
        in_specs=[pl.BlockSpec((1, tq, HEAD_PAD), lambda bi, h, i: (bi, i, h)),
                  pl.BlockSpec((1, s_total, HEAD_PAD), lambda bi, h, i: (bi, 0, h)),
                  pl.BlockSpec((1, s_total, HEAD_PAD), lambda bi, h, i: (bi, 0, h))],
        out_specs=pl.BlockSpec((1, tq, HEAD_PAD), lambda bi, h, i: (bi, i, h)),
        compiler_params=_params(3),
        name="attn",
    )(q, k_all, v_all)


def _mix_out_kernel(x_ref, yf_ref, yb_ref, z_ref, a_ref, m_ref, sn_ref, ws_ref, wm_ref, o_ref):
    y = (yf_ref[...] + yb_ref[...]) * _silu(z_ref[...])
    ssd = _rms(y, sn_ref[...]).astype(BF16)
    yx = _dot(ssd, ws_ref[...]) + _dot(a_ref[...], wm_ref[...])
    o_ref[...] = x_ref[...] + m_ref[0][5:6] * yx


def _mix_out(x2d, yf, yb, z, mla, m_all, m_index, sn, ws, wm, tm):
    n, d = x2d.shape
    tok = lambda c: pl.BlockSpec((tm, c), lambda i: (i, 0))
    return pl.pallas_call(
        _mix_out_kernel,
        out_shape=jax.ShapeDtypeStruct((n, d), F32),
        grid=(n // tm,),
        in_specs=[tok(d), tok(SSD_WIDTH), tok(SSD_WIDTH), tok(SSD_WIDTH), tok(MLA_PAD),
                  pl.BlockSpec((1, N_MOD, d), lambda i: (m_index(i), 0, 0)),
                  _const_spec(sn.shape), _const_spec(ws.shape), _const_spec(wm.shape)],
        out_specs=tok(d),
        compiler_params=_params(1),
        name="mix_out",
    )(x2d, yf, yb, z, mla, m_all, sn, ws, wm)


def _rope_swap(w):
    parts = []
    for base in (0, QK_ROPE // 2):
        u1 = w[..., base:base + QK_ROPE // 4]
        u2 = w[..., base + QK_ROPE // 4:base + QK_ROPE // 2]
        parts += [-u2, u1]
    return jnp.concatenate(parts, axis=-1)


def _rope_tables(l_x):
    half = QK_ROPE // 2
    inv = ROPE_BASE ** (-jnp.arange(0, half, 2, dtype=F32) / half)
    rows = l_x // GRID_W
    row = jnp.repeat(jnp.arange(rows, dtype=jnp.int32), GRID_W)
    col = jnp.arange(rows * GRID_W, dtype=jnp.int32) % GRID_W
    ar = row.astype(F32)[:, None] * inv
    ac = col.astype(F32)[:, None] * inv
    cos = jnp.concatenate([jnp.cos(ar), jnp.cos(ar), jnp.cos(ac), jnp.cos(ac)], axis=1)
    sin = jnp.concatenate([jnp.sin(ar), jnp.sin(ar), jnp.sin(ac), jnp.sin(ac)], axis=1)
    return cos, sin


def _pick_tile(n, pref):
    t = min(pref, n)
    while n % t:
        t //= 2
    return t


def kernel(x, c, ctx, c_ctx, w_ada, b_ada, norm_ffn1, w_ffn1_in, w_ffn1_out, norm_mix, w_in, conv_w, conv_b, dt_bias, a_log, d_skip, ssd_norm, q_norm, w_uq, kv_norm, w_ukv, w_out, norm_ffn2, w_ffn2_in, w_ffn2_out, final_norm):
    b, l_x, d = x.shape
    l_c = ctx.shape[1]
    assert d == D_MODEL and w_ada.shape[0] == 1
    assert l_x % SSD_CHUNK == 0 and l_c % SSD_CHUNK == 0 and l_x % GRID_W == 0
    s_total = l_x + l_c

    c_rows = jnp.concatenate([c, c_ctx[None, :]], axis=0)
    pad = (-c_rows.shape[0]) % SUBLANES
    c_rows = jnp.pad(c_rows, ((0, pad), (0, 0)))
    m_all = _ada(c_rows, w_ada[0], b_ada)[:b + 1].reshape(b + 1, N_MOD, d)

    tm = _pick_tile(l_x, 512)
    tmc = _pick_tile(l_c, 256)
    x_tiles = l_x // tm
    m_lat = lambda i: i // x_tiles
    m_ctx = lambda i: b

    wi1, wo1 = w_ffn1_in[0].astype(BF16), w_ffn1_out[0].astype(BF16)
    g1 = norm_ffn1
    x1 = _ffn(x.reshape(b * l_x, d), m_all, m_lat, g1, wi1, wo1, 0, tm)
    c1 = _ffn(ctx.reshape(b * l_c, d), m_all, m_ctx, g1, wi1, wo1, 0, tmc)

    w = w_in[0]
    w_kpe = w[:, SPLIT_CKV:]
    small = jnp.concatenate([w_kpe, _rope_swap(w_kpe), w[:, SPLIT_XBC:SPLIT_DT],
                             jnp.zeros((d, SMALL_W - 2 * QK_ROPE - N_DIR * SSD_HEADS), F32)], axis=1)
    w_all = jnp.concatenate([w[:, :SPLIT_XBC], w[:, SPLIT_DT:SPLIT_CKV], small], axis=1).astype(BF16)
    hq = QK_NOPE + QK_ROPE
    wq = w_uq[0].reshape(Q_LORA, MLA_HEADS, hq)
    zq = jnp.zeros((Q_LORA, MLA_HEADS, HEAD_PAD - hq), F32)
    wqa = jnp.concatenate([wq, zq], axis=2).reshape(Q_LORA, MLA_PAD).astype(BF16)
    wqb = jnp.concatenate([jnp.zeros((Q_LORA, MLA_HEADS, QK_NOPE), F32), _rope_swap(wq[:, :, QK_NOPE:]), zq],
                          axis=2).reshape(Q_LORA, MLA_PAD).astype(BF16)
    wkv = w_ukv[0].reshape(KV_LORA, MLA_HEADS, QK_NOPE + V_DIM)
    wuk = jnp.concatenate([wkv[:, :, :QK_NOPE], jnp.zeros((KV_LORA, MLA_HEADS, HEAD_PAD - QK_NOPE), F32)],
                          axis=2).reshape(KV_LORA, MLA_PAD).astype(BF16)
    wuv = jnp.concatenate([wkv[:, :, QK_NOPE:], jnp.zeros((KV_LORA, MLA_HEADS, HEAD_PAD - V_DIM), F32)],
                          axis=2).reshape(KV_LORA, MLA_PAD).astype(BF16)
    e_np = np.zeros((SMALL_W, MLA_HEADS, HEAD_PAD), np.float32)
    for i in range(QK_ROPE):
        e_np[i, :, QK_NOPE + i] = 1.0
    e_mat = jnp.asarray(e_np.reshape(SMALL_W, MLA_PAD), BF16)

    cos, sin = _rope_tables(l_x)
    zpad = lambda n: jnp.zeros((l_x, n), F32)
    tqc = MLA_SCALE * jnp.concatenate([jnp.ones((l_x, QK_NOPE), F32), cos, zpad(HEAD_PAD - hq)], axis=1)
    tqs = MLA_SCALE * jnp.concatenate([zpad(QK_NOPE), sin, zpad(HEAD_PAD - hq)], axis=1)
    tk_lat = jnp.concatenate([cos, sin, zpad(SMALL_W - 2 * QK_ROPE)], axis=1)
    tk_ctx = jnp.concatenate([jnp.ones((l_c, QK_ROPE), F32), jnp.zeros((l_c, SMALL_W - QK_ROPE), F32)], axis=1)

    shared = (norm_mix, w_all, kv_norm, wuk, wuv, e_mat)
    tmi = _pick_tile(l_x, 512)
    xbc_all, small_all, k_all, v_all, z, q = _in_proj(
        x1.reshape(b, l_x, d), m_all, lambda bi: bi, *shared, tk_lat, tmi, 0, s_total,
        q_args=(q_norm, wqa, wqb, tqc, tqs))
    tmic = _pick_tile(math.gcd(l_c, l_x), 256)
    xbc_all, small_all, k_all, v_all = _in_proj(
        c1.reshape(b, l_c, d), m_all, lambda bi: b, *shared, tk_ctx, tmic, l_x, s_total,
        alias_bufs=(xbc_all, small_all, k_all, v_all))

    cw = jnp.pad(conv_w[0], ((0, SUBLANES - SSD_CONV), (0, 0)))
    lane_pad = lambda v: jnp.pad(v.reshape(1, -1), ((0, 0), (DT_LANE0, SMALL_W - DT_LANE0 - v.size)))
    dtb = lane_pad(dt_bias[0])
    aneg = lane_pad(-jnp.exp(a_log[0].astype(F32)))
    dskip = jnp.repeat(d_skip[0], SSD_HEAD_DIM)[None, :]
    yf = _ssd(xbc_all, small_all, cw, conv_b, dtb, aneg, dskip, 0, l_x, l_c)
    yb = _ssd(xbc_all, small_all, cw, conv_b, dtb, aneg, dskip, 1, l_x, l_c)

    mla = _attn(q, k_all, v_all, _pick_tile(l_x, 512))

    wo = w_out[0]
    ws = wo[:SSD_WIDTH].astype(BF16)
    wm = jnp.concatenate([wo[SSD_WIDTH:].reshape(MLA_HEADS, V_DIM, d),
                          jnp.zeros((MLA_HEADS, HEAD_PAD - V_DIM, d), F32)], axis=1).reshape(MLA_PAD, d).astype(BF16)
    n = b * l_x
    x2 = _mix_out(x1, yf.reshape(n, SSD_WIDTH), yb.reshape(n, SSD_WIDTH), z.reshape(n, SSD_WIDTH),
                  mla.reshape(n, MLA_PAD), m_all, m_lat, ssd_norm, ws, wm, tm)
    out = _ffn(x2, m_all, m_lat, norm_ffn2, w_ffn2_in[0].astype(BF16), w_ffn2_out[0].astype(BF16), 6, tm,
               final_g=final_norm[None, :])
    return out.reshape(b, l_x, d)
```

```python
import functools
import math

import jax
import jax.numpy as jnp
import numpy as np
from jax import lax
from jax.experimental import pallas as pl
from jax.experimental.pallas import tpu as pltpu

F32 = jnp.float32
BF16 = jnp.bfloat16

D_MODEL = 1024
GRID_W = 64
N_MOD = 9
D_FF = 2816
EPS = 1e-6
SSD_HEADS = 8
SSD_HEAD_DIM = 64
SSD_WIDTH = SSD_HEADS * SSD_HEAD_DIM
SSD_GROUPS = 2
SSD_STATE = 128
SSD_CONV = 5
SSD_CHUNK = 128
N_DIR = 2
CONV_CH = SSD_WIDTH + 2 * SSD_GROUPS * SSD_STATE
MLA_HEADS = 8
QK_NOPE = 64
QK_ROPE = 32
V_DIM = 64
Q_LORA = 768
KV_LORA = 256
MLA_SCALE = (QK_NOPE + QK_ROPE) ** -0.5
ROPE_BASE = 10000.0
SPLIT_Z = SSD_WIDTH
SPLIT_XBC = SPLIT_Z + CONV_CH
SPLIT_DT = SPLIT_XBC + N_DIR * SSD_HEADS
SPLIT_CQ = SPLIT_DT + Q_LORA
SPLIT_CKV = SPLIT_CQ + KV_LORA

LANES = 128
SUBLANES = 8
HEAD_PAD = 128
MLA_PAD = MLA_HEADS * HEAD_PAD
FF_CHUNK = 256
SMALL_W = LANES
DT_LANE0 = 2 * QK_ROPE
W_ALL = SSD_WIDTH + CONV_CH + Q_LORA + KV_LORA + SMALL_W
VMEM_LIMIT = 56 * 1024 * 1024


def _dot(a, b):
    return jnp.dot(a, b, preferred_element_type=F32)


def _dot_nt(a, b):
    return lax.dot_general(a, b, (((1,), (1,)), ((), ())), preferred_element_type=F32)


def _dot_tn(a, b):
    return lax.dot_general(a, b, (((0,), (0,)), ((), ())), preferred_element_type=F32)


def _silu(x):
    return x * (1.0 / (1.0 + jnp.exp(-x)))


def _rms(x, g):
    return x * lax.rsqrt(jnp.mean(x * x, axis=-1, keepdims=True) + EPS) * g


def _const_spec(shape):
    zeros = (0,) * len(shape)
    return pl.BlockSpec(shape, lambda *_: zeros, pipeline_mode=pl.Buffered(1))


def _params(n_axes):
    return pltpu.CompilerParams(dimension_semantics=("arbitrary",) * n_axes,
                                vmem_limit_bytes=VMEM_LIMIT)


def _ada_kernel(c_ref, w_ref, b_ref, o_ref):
    s = _silu(c_ref[...]).astype(BF16)
    o_ref[...] = _dot(s, w_ref[...].astype(BF16)) + b_ref[...]


def _ada(c_rows, w, b):
    rows, d = c_rows.shape
    n = w.shape[1]
    tn = 1024
    return pl.pallas_call(
        _ada_kernel,
        out_shape=jax.ShapeDtypeStruct((rows, n), F32),
        grid=(n // tn,),
        in_specs=[pl.BlockSpec((rows, d), lambda j: (0, 0)),
                  pl.BlockSpec((d, tn), lambda j: (0, j)),
                  pl.BlockSpec((1, tn), lambda j: (0, j))],
        out_specs=pl.BlockSpec((rows, tn), lambda j: (0, j)),
        compiler_params=_params(1),
        name="ada",
    )(c_rows, w, b)


def _ffn_kernel(x_ref, m_ref, g_ref, wi_ref, wo_ref, *rest, k, final):
    x = x_ref[...]
    m = m_ref[0]
    if final:
        yf_ref, yb_ref, z_ref, a_ref, sn_ref, ws_ref, wm_ref, fg_ref, o_ref, acc_ref = rest
        y = (yf_ref[...] + yb_ref[...]) * _silu(z_ref[...])
        ssd = _rms(y, sn_ref[...]).astype(BF16)
        x = x + m[k - 1:k] * (_dot(ssd, ws_ref[...]) + _dot(a_ref[...], wm_ref[...]))
    else:
        o_ref, acc_ref = rest
    shift, scale, gate = m[k:k + 1], m[k + 1:k + 2], m[k + 2:k + 3]
    hn = (_rms(x, g_ref[...]) * (1 + scale) + shift).astype(BF16)
    for j in range(D_FF // FF_CHUNK):
        lo = j * FF_CHUNK
        g = _dot(hn, wi_ref[:, lo:lo + FF_CHUNK])
        u = _dot(hn, wi_ref[:, D_FF + lo:D_FF + lo + FF_CHUNK])
        a = (_silu(g) * u).astype(BF16)
        part = _dot(a, wo_ref[lo:lo + FF_CHUNK, :])
        if j == 0:
            acc_ref[...] = part
        else:
            acc_ref[...] += part
    y = x + 0.5 * gate * acc_ref[...]
    if final:
        y = _rms(y, fg_ref[...])
    o_ref[...] = y


def _ffn(x2d, m_all, m_index, g, wi, wo, k, tm, mix=None):
    n, d = x2d.shape
    final = mix is not None
    tok = lambda c: pl.BlockSpec((tm, c), lambda i: (i, 0))
    in_specs = [tok(d),
                pl.BlockSpec((1, N_MOD, d), lambda i: (m_index(i), 0, 0)),
                _const_spec((1, d)),
                _const_spec(wi.shape),
                _const_spec(wo.shape)]
    args = [x2d, m_all, g, wi, wo]
    if final:
        in_specs += [tok(a.shape[1]) for a in mix[:4]] + [_const_spec(a.shape) for a in mix[4:]]
        args += list(mix)
    return pl.pallas_call(
        functools.partial(_ffn_kernel, k=k, final=final),
        out_shape=jax.ShapeDtypeStruct((n, d), F32),
        grid=(n // tm,),
        in_specs=in_specs,
        out_specs=pl.BlockSpec((tm, d), lambda i: (i, 0)),
        scratch_shapes=[pltpu.VMEM((tm, d), F32)],
        compiler_params=_params(1),
        name="ffn_final" if final else "ffn",
    )(*args)


def _in_proj_kernel(x_ref, m_ref, g_ref, w_ref, kvn_ref, wuk_ref, wuv_ref, e_ref, tk_ref,
                    *rest, with_q, n_alias):
    if with_q:
        qn_ref, wqa_ref, wqb_ref, tqc_ref, tqs_ref = rest[:5]
        rest = rest[5:]
    rest = rest[n_alias:]
    xbc_ref, small_ref, k_ref, v_ref = rest[:4]
    if with_q:
        z_ref, q_ref = rest[4:6]
    m = m_ref[0]
    hn = (_rms(x_ref[0], g_ref[...]) * (1 + m[4:5]) + m[3:4]).astype(BF16)
    o_xbc = SSD_WIDTH
    o_cq = o_xbc + CONV_CH
    o_ckv = o_cq + Q_LORA
    o_small = o_ckv + KV_LORA
    xbc_ref[0] = _dot(hn, w_ref[:, o_xbc:o_cq])
    small = _dot(hn, w_ref[:, o_small:o_small + SMALL_W])
    small_ref[0] = small
    ckv = _rms(_dot(hn, w_ref[:, o_ckv:o_small]), kvn_ref[...]).astype(BF16)
    t = small * tk_ref[...]
    kpe = (t + pltpu.roll(t, LANES - QK_ROPE, axis=1)).astype(BF16)
    k_ref[0] = (_dot(ckv, wuk_ref[...]) + _dot(kpe, e_ref[...])).astype(BF16)
    v_ref[0] = _dot(ckv, wuv_ref[...]).T.astype(BF16)
    if with_q:
        z_ref[0] = _dot(hn, w_ref[:, 0:o_xbc])
        cq = _rms(_dot(hn, w_ref[:, o_cq:o_ckv]), qn_ref[...]).astype(BF16)
        tqc = jnp.concatenate([tqc_ref[...]] * MLA_HEADS, axis=1)
        tqs = jnp.concatenate([tqs_ref[...]] * MLA_HEADS, axis=1)
        q_ref[0] = (_dot(cq, wqa_ref[...]) * tqc + _dot(cq, wqb_ref[...]) * tqs).astype(BF16)


def _in_proj(h3d, m_all, m_index, g, w_all, kvn, wuk, wuv, e_mat, tk, tm, row0, s_total,
             q_args=None, alias_bufs=()):
    b, rows, d = h3d.shape
    with_q = q_args is not None
    blk0 = row0 // tm
    tok = lambda c: pl.BlockSpec((1, tm, c), lambda bi, i: (bi, i, 0))
    comb = lambda c: pl.BlockSpec((1, tm, c), lambda bi, i: (bi, blk0 + i, 0))
    in_specs = [tok(d),
                pl.BlockSpec((1, N_MOD, d), lambda bi, i: (m_index(bi), 0, 0)),
                _const_spec(g.shape), _const_spec(w_all.shape), _const_spec(kvn.shape),
                _const_spec(wuk.shape), _const_spec(wuv.shape), _const_spec(e_mat.shape),
                pl.BlockSpec((tm, LANES), lambda bi, i: (i, 0))]
    args = [h3d, m_all, g, w_all, kvn, wuk, wuv, e_mat, tk]
    out_shape = [jax.ShapeDtypeStruct((b, s_total, CONV_CH), F32),
                 jax.ShapeDtypeStruct((b, s_total, SMALL_W), F32),
                 jax.ShapeDtypeStruct((b, s_total, MLA_PAD), BF16),
                 jax.ShapeDtypeStruct((b, MLA_HEADS * V_DIM, s_total), BF16)]
    out_specs = [comb(CONV_CH), comb(SMALL_W), comb(MLA_PAD),
                 pl.BlockSpec((1, MLA_HEADS * V_DIM, tm), lambda bi, i: (bi, 0, blk0 + i))]
    if with_q:
        qn, wqa, wqb, tqc, tqs = q_args
        in_specs += [_const_spec(qn.shape), _const_spec(wqa.shape), _const_spec(wqb.shape),
                     pl.BlockSpec((tm, LANES), lambda bi, i: (i, 0)),
                     pl.BlockSpec((tm, LANES), lambda bi, i: (i, 0))]
        args += [qn, wqa, wqb, tqc, tqs]
        out_shape += [jax.ShapeDtypeStruct((b, rows, SSD_WIDTH), F32),
                      jax.ShapeDtypeStruct((b, rows, MLA_PAD), BF16)]
        out_specs += [tok(SSD_WIDTH), tok(MLA_PAD)]
    aliases = {len(args) + j: j for j in range(len(alias_bufs))}
    in_specs += [pl.BlockSpec(memory_space=pl.ANY)] * len(alias_bufs)
    args += list(alias_bufs)
    return pl.pallas_call(
        functools.partial(_in_proj_kernel, with_q=with_q, n_alias=len(alias_bufs)),
        out_shape=out_shape,
        grid=(b, rows // tm),
        in_specs=in_specs,
        out_specs=out_specs,
        input_output_aliases=aliases,
        compiler_params=_params(2),
        name="in_proj_q" if with_q else "in_proj_ctx",
    )(*args)


def _tile_rows(blk, reps):
    return jnp.concatenate([blk] * reps, axis=0)


def _ssd_kernel(cur_ref, prev_ref, next_ref, small_ref, cw_ref, cb_ref, dtb_ref, aneg_ref,
                dskip_ref, y_ref, h_ref, *, direction, ncx, ncc):
    q = SSD_CHUNK
    t = pl.program_id(1)
    c = jnp.where(t < ncc, ncx + t, t - ncc) if direction == 0 else ncx + ncc - 1 - t

    @pl.when(t == 0)
    def _():
        h_ref[...] = jnp.zeros_like(h_ref)

    has_prev = jnp.logical_and(c != 0, c != ncx).astype(F32)
    has_next = jnp.logical_and(c != ncx - 1, c != ncx + ncc - 1).astype(F32)
    cur = cur_ref[0]
    prev = prev_ref[0] * has_prev
    nxt = next_ref[0] * has_next
    reps = q // SUBLANES
    row = lax.broadcasted_iota(jnp.int32, (q, CONV_CH), 0)
    xm2 = jnp.where(row < 2, _tile_rows(pltpu.roll(prev, 2, axis=0), reps), pltpu.roll(cur, 2, axis=0))
    xm1 = jnp.where(row < 1, _tile_rows(pltpu.roll(prev, 1, axis=0), reps), pltpu.roll(cur, 1, axis=0))
    xp1 = jnp.where(row >= q - 1, _tile_rows(pltpu.roll(nxt, SUBLANES - 1, axis=0), reps),
                    pltpu.roll(cur, q - 1, axis=0))
    xp2 = jnp.where(row >= q - 2, _tile_rows(pltpu.roll(nxt, SUBLANES - 2, axis=0), reps),
                    pltpu.roll(cur, q - 2, axis=0))
    cw = cw_ref[...]
    u = (xm2 * cw[0:1] + xm1 * cw[1:2] + cur * cw[2:3] + xp1 * cw[3:4] + xp2 * cw[4:5]) + cb_ref[...]
    u = _silu(u)
    xs = u[:, 0:SSD_WIDTH]
    o_b = SSD_WIDTH
    o_c = SSD_WIDTH + SSD_GROUPS * SSD_STATE
    bm = u[:, o_b:o_c].astype(BF16)
    cm = u[:, o_c:CONV_CH].astype(BF16)

    pre = small_ref[0] + dtb_ref[...]
    dtv = jnp.maximum(pre, 0.0) + jnp.log(1.0 + jnp.exp(-jnp.abs(pre)))
    a = dtv * aneg_ref[...]
    ri = lax.broadcasted_iota(jnp.int32, (q, q), 0)
    ci = lax.broadcasted_iota(jnp.int32, (q, q), 1)
    keep = (ri >= ci) if direction == 0 else (ri <= ci)
    tri = jnp.where(keep, 1.0, 0.0).astype(BF16)
    a1 = a.astype(BF16)
    r1 = a - a1.astype(F32)
    a2 = r1.astype(BF16)
    a3 = (r1 - a2.astype(F32)).astype(BF16)
    acum = _dot(tri, a1) + _dot(tri, a2) + _dot(tri, a3)
    acum_t = acum.T
    last = q - 1 if direction == 0 else 0
    dend = jnp.exp(acum[last:last + 1, :] - acum)
    dstart = jnp.exp(acum)

    lane = lax.broadcasted_iota(jnp.int32, (q, LANES), 1)
    head_lane = lambda h: DT_LANE0 + SSD_HEADS * direction + h

    def expand(arr):
        cols = []
        for j in range(SSD_HEADS // 2):
            la, lb = head_lane(2 * j), head_lane(2 * j + 1)
            cols.append(jnp.where(lane < SSD_HEAD_DIM, arr[:, la:la + 1], arr[:, lb:lb + 1]))
        return jnp.concatenate(cols, axis=1)

    dt_x = expand(dtv)
    dend_x = expand(dend)
    dstart_x = expand(dstart)
    xdt = xs * dt_x
    xend = (xdt * dend_x).astype(BF16)
    gw = SSD_WIDTH // SSD_GROUPS
    hpg = SSD_HEADS // SSD_GROUPS
    glane = lax.broadcasted_iota(jnp.int32, (q, gw), 1)
    ys = []
    for g in range(SSD_GROUPS):
        cg = cm[:, g * SSD_STATE:(g + 1) * SSD_STATE]
        bg = bm[:, g * SSD_STATE:(g + 1) * SSD_STATE]
        cbm = _dot_nt(cg, bg)
        xg = xdt[:, g * gw:(g + 1) * gw]
        yg = None
        for hh in range(hpg):
            hl = head_lane(g * hpg + hh)
            seg = acum[:, hl:hl + 1] - acum_t[hl:hl + 1, :]
            lmat = (cbm * jnp.exp(jnp.where(keep, seg, -jnp.inf))).astype(BF16)
            in_head = jnp.logical_and(glane >= hh * SSD_HEAD_DIM, glane < (hh + 1) * SSD_HEAD_DIM)
            part = _dot(lmat, jnp.where(in_head, xg, 0.0).astype(BF16))
            yg = part if yg is None else yg + part
        hst = h_ref[g]
        yoff = _dot(cg, hst.astype(BF16)) * dstart_x[:, g * gw:(g + 1) * gw]
        ys.append(yg + yoff)
        h_ref[g] = dstart_x[last:last + 1, g * gw:(g + 1) * gw] * hst + _dot_tn(bg, xend[:, g * gw:(g + 1) * gw])

    @pl.when(t >= ncc)
    def _():
        y = jnp.concatenate(ys, axis=1)
        if direction == 0:
            y = y + dskip_ref[...] * xs
        y_ref[0] = y


def _ssd(xbc_all, small_all, cw, cb, dtb, aneg, dskip, direction, l_x, l_c):
    b, s_total, _ = xbc_all.shape
    q = SSD_CHUNK
    ncx, ncc = l_x // q, l_c // q
    hb = q // SUBLANES
    n8 = s_total // SUBLANES

    def chunk(t):
        if direction == 0:
            return jnp.where(t < ncc, ncx + t, t - ncc)
        return ncx + ncc - 1 - t

    def xchunk(t):
        return chunk(jnp.maximum(t, ncc))

    return pl.pallas_call(
        functools.partial(_ssd_kernel, direction=direction, ncx=ncx, ncc=ncc),
        out_shape=jax.ShapeDtypeStruct((b, l_x, SSD_WIDTH), F32),
        grid=(b, ncx + ncc),
        in_specs=[pl.BlockSpec((1, q, CONV_CH), lambda bi, t: (bi, chunk(t), 0)),
                  pl.BlockSpec((1, SUBLANES, CONV_CH),
                               lambda bi, t: (bi, jnp.maximum(chunk(t) * hb - 1, 0), 0)),
                  pl.BlockSpec((1, SUBLANES, CONV_CH),
                               lambda bi, t: (bi, jnp.minimum(chunk(t) * hb + hb, n8 - 1), 0)),
                  pl.BlockSpec((1, q, SMALL_W), lambda bi, t: (bi, chunk(t), 0)),
                  _const_spec(cw.shape), _const_spec(cb.shape), _const_spec(dtb.shape),
                  _const_spec(aneg.shape), _const_spec(dskip.shape)],
        out_specs=pl.BlockSpec((1, q, SSD_WIDTH), lambda bi, t: (bi, xchunk(t), 0)),
        scratch_shapes=[pltpu.VMEM((SSD_GROUPS, SSD_STATE, SSD_WIDTH // SSD_GROUPS), F32)],
        compiler_params=_params(2),
        name="ssd_fwd" if direction == 0 else "ssd_bwd",
    )(xbc_all, xbc_all, xbc_all, small_all, cw, cb, dtb, aneg, dskip)


HEADS_PER_STEP = LANES // V_DIM


def _attn_kernel(q_ref, k_ref, vt_ref, o_ref):
    outs = []
    for hh in range(HEADS_PER_STEP):
        q = q_ref[0, :, hh * HEAD_PAD:(hh + 1) * HEAD_PAD]
        k = k_ref[0, :, hh * HEAD_PAD:(hh + 1) * HEAD_PAD]
        s = _dot_nt(k, q)
        m = jnp.max(s, axis=0, keepdims=True)
        p = jnp.exp2(s - m)
        l = jnp.sum(p, axis=0, keepdims=True)
        ot = _dot(vt_ref[0, hh * V_DIM:(hh + 1) * V_DIM, :], p.astype(BF16))
        outs.append((ot / l).T)
    o_ref[0] = jnp.concatenate(outs, axis=1).astype(BF16)


def _attn(q, k_all, vt_all, tq):
    b, l_x, _ = q.shape
    s_total = k_all.shape[1]
    hw = HEADS_PER_STEP * HEAD_PAD
    return pl.pallas_call(
        _attn_kernel,
        out_shape=jax.ShapeDtypeStruct((b, l_x, MLA_HEADS * V_DIM), BF16),
        grid=(b, MLA_HEADS // HEADS_PER_STEP, l_x // tq),
        in_specs=[pl.BlockSpec((1, tq, hw), lambda bi, h, i: (bi, i, h)),
                  pl.BlockSpec((1, s_total, hw), lambda bi, h, i: (bi, 0, h)),
                  pl.BlockSpec((1, LANES, s_total), lambda bi, h, i: (bi, h, 0))],
        out_specs=pl.BlockSpec((1, tq, LANES), lambda bi, h, i: (bi, i, h)),
        compiler_params=_params(3),
        name="attn",
    )(q, k_all, vt_all)


def _rope_swap(w):
    parts = []
    for base in (0, QK_ROPE // 2):
        u1 = w[..., base:base + QK_ROPE // 4]
        u2 = w[..., base + QK_ROPE // 4:base + QK_ROPE // 2]
        parts += [-u2, u1]
    return jnp.concatenate(parts, axis=-1)


def _rope_tables(l_x):
    half = QK_ROPE // 2
    inv = (ROPE_BASE ** (-np.arange(0, half, 2, dtype=np.float32) / half)).astype(np.float32)
    rows = l_x // GRID_W
    row = np.repeat(np.arange(rows), GRID_W).astype(np.float32)
    col = (np.arange(rows * GRID_W) % GRID_W).astype(np.float32)
    ar = (row[:, None] * inv).astype(np.float64)
    ac = (col[:, None] * inv).astype(np.float64)
    cos = np.concatenate([np.cos(ar), np.cos(ar), np.cos(ac), np.cos(ac)], axis=1).astype(np.float32)
    sin = np.concatenate([np.sin(ar), np.sin(ar), np.sin(ac), np.sin(ac)], axis=1).astype(np.float32)
    return cos, sin


def _pick_tile(n, pref):
    t = min(pref, n)
    while n % t:
        t //= 2
    return t


def kernel(x, c, ctx, c_ctx, w_ada, b_ada, norm_ffn1, w_ffn1_in, w_ffn1_out, norm_mix, w_in, conv_w, conv_b, dt_bias, a_log, d_skip, ssd_norm, q_norm, w_uq, kv_norm, w_ukv, w_out, norm_ffn2, w_ffn2_in, w_ffn2_out, final_norm):
    b, l_x, d = x.shape
    l_c = ctx.shape[1]
    assert d == D_MODEL and w_ada.shape[0] == 1
    assert l_x % SSD_CHUNK == 0 and l_c % SSD_CHUNK == 0 and l_x % GRID_W == 0
    s_total = l_x + l_c

    c_rows = jnp.concatenate([c, c_ctx[None, :]], axis=0)
    pad = (-c_rows.shape[0]) % SUBLANES
    c_rows = jnp.pad(c_rows, ((0, pad), (0, 0)))
    m_all = _ada(c_rows, w_ada[0], b_ada)[:b + 1].reshape(b + 1, N_MOD, d)

    tm = _pick_tile(l_x, 512)
    tmc = _pick_tile(l_c, 256)
    x_tiles = l_x // tm
    m_lat = lambda i: i // x_tiles
    m_ctx = lambda i: b

    wi1, wo1 = w_ffn1_in[0].astype(BF16), w_ffn1_out[0].astype(BF16)
    g1 = norm_ffn1
    x1 = _ffn(x.reshape(b * l_x, d), m_all, m_lat, g1, wi1, wo1, 0, tm)
    c1 = _ffn(ctx.reshape(b * l_c, d), m_all, m_ctx, g1, wi1, wo1, 0, tmc)

    w = w_in[0]
    w_kpe = w[:, SPLIT_CKV:]
    small = jnp.concatenate([w_kpe, _rope_swap(w_kpe), w[:, SPLIT_XBC:SPLIT_DT],
                             jnp.zeros((d, SMALL_W - 2 * QK_ROPE - N_DIR * SSD_HEADS), F32)], axis=1)
    w_all = jnp.concatenate([w[:, :SPLIT_XBC], w[:, SPLIT_DT:SPLIT_CKV], small], axis=1).astype(BF16)
    hq = QK_NOPE + QK_ROPE
    wq = w_uq[0].reshape(Q_LORA, MLA_HEADS, hq)
    zq = jnp.zeros((Q_LORA, MLA_HEADS, HEAD_PAD - hq), F32)
    wqa = jnp.concatenate([wq, zq], axis=2).reshape(Q_LORA, MLA_PAD).astype(BF16)
    wqb = jnp.concatenate([jnp.zeros((Q_LORA, MLA_HEADS, QK_NOPE), F32), _rope_swap(wq[:, :, QK_NOPE:]), zq],
                          axis=2).reshape(Q_LORA, MLA_PAD).astype(BF16)
    wkv = w_ukv[0].reshape(KV_LORA, MLA_HEADS, QK_NOPE + V_DIM)
    wuk = jnp.concatenate([wkv[:, :, :QK_NOPE], jnp.zeros((KV_LORA, MLA_HEADS, HEAD_PAD - QK_NOPE), F32)],
                          axis=2).reshape(KV_LORA, MLA_PAD).astype(BF16)
    wuv = wkv[:, :, QK_NOPE:].reshape(KV_LORA, MLA_HEADS * V_DIM).astype(BF16)
    e_np = np.zeros((SMALL_W, MLA_HEADS, HEAD_PAD), np.float32)
    for i in range(QK_ROPE):
        e_np[i, :, QK_NOPE + i] = 1.0
    e_mat = jnp.asarray(e_np.reshape(SMALL_W, MLA_PAD), BF16)

    cos, sin = _rope_tables(l_x)
    zpad = lambda n: np.zeros((l_x, n), np.float32)
    q_scale = np.float32(MLA_SCALE * math.log2(math.e))
    tqc = jnp.asarray(q_scale * np.concatenate([np.ones((l_x, QK_NOPE), np.float32), cos, zpad(HEAD_PAD - hq)], axis=1))
    tqs = jnp.asarray(q_scale * np.concatenate([zpad(QK_NOPE), sin, zpad(HEAD_PAD - hq)], axis=1))
    tk_lat = jnp.asarray(np.concatenate([cos, sin, zpad(SMALL_W - 2 * QK_ROPE)], axis=1))
    tk_ctx = jnp.asarray(np.concatenate([np.ones((l_c, QK_ROPE), np.float32),
                                         np.zeros((l_c, SMALL_W - QK_ROPE), np.float32)], axis=1))

    shared = (norm_mix, w_all, kv_norm, wuk, wuv, e_mat)
    tmi = _pick_tile(l_x, 512)
    xbc_all, small_all, k_all, v_all, z, q = _in_proj(
        x1.reshape(b, l_x, d), m_all, lambda bi: bi, *shared, tk_lat, tmi, 0, s_total,
        q_args=(q_norm, wqa, wqb, tqc, tqs))
    tmic = _pick_tile(math.gcd(l_c, l_x), 256)
    xbc_all, small_all, k_all, v_all = _in_proj(
        c1.reshape(b, l_c, d), m_all, lambda bi: b, *shared, tk_ctx, tmic, l_x, s_total,
        alias_bufs=(xbc_all, small_all, k_all, v_all))

    cw = jnp.pad(conv_w[0], ((0, SUBLANES - SSD_CONV), (0, 0)))
    lane_pad = lambda v: jnp.pad(v.reshape(1, -1), ((0, 0), (DT_LANE0, SMALL_W - DT_LANE0 - v.size)))
    dtb = lane_pad(dt_bias[0])
    aneg = lane_pad(-jnp.exp(a_log[0].astype(F32)))
    dskip = jnp.repeat(d_skip[0], SSD_HEAD_DIM)[None, :]
    yf = _ssd(xbc_all, small_all, cw, conv_b, dtb, aneg, dskip, 0, l_x, l_c)
    yb = _ssd(xbc_all, small_all, cw, conv_b, dtb, aneg, dskip, 1, l_x, l_c)

    mla = _attn(q, k_all, v_all, _pick_tile(l_x, 512))

    wo = w_out[0]
    ws = wo[:SSD_WIDTH].astype(BF16)
    wm = wo[SSD_WIDTH:].astype(BF16)
    n = b * l_x
    mix = (yf.reshape(n, SSD_WIDTH), yb.reshape(n, SSD_WIDTH), z.reshape(n, SSD_WIDTH),
           mla.reshape(n, MLA_HEADS * V_DIM), ssd_norm, ws, wm, final_norm[None, :])
    out = _ffn(x1, m_all, m_lat, norm_ffn2, w_ffn2_in[0].astype(BF16), w_ffn2_out[0].astype(BF16), 6, tm,
               mix=mix)
    return out.reshape(b, l_x, d)
```

```python
import functools
import math

import jax
import jax.numpy as jnp
import numpy as np
from jax import lax
from jax.experimental import pallas as pl
from jax.experimental.pallas import tpu as pltpu

F32 = jnp.float32
BF16 = jnp.bfloat16

D_MODEL = 1024
GRID_W = 64
N_MOD = 9
D_FF = 2816
EPS = 1e-6
SSD_HEADS = 8
SSD_HEAD_DIM = 64
SSD_WIDTH = SSD_HEADS * SSD_HEAD_DIM
SSD_GROUPS = 2
SSD_STATE = 128
SSD_CONV = 5
SSD_CHUNK = 128
N_DIR = 2
CONV_CH = SSD_WIDTH + 2 * SSD_GROUPS * SSD_STATE
MLA_HEADS = 8
QK_NOPE = 64
QK_ROPE = 32
V_DIM = 64
Q_LORA = 768
KV_LORA = 256
MLA_SCALE = (QK_NOPE + QK_ROPE) ** -0.5
ROPE_BASE = 10000.0
SPLIT_Z = SSD_WIDTH
SPLIT_XBC = SPLIT_Z + CONV_CH
SPLIT_DT = SPLIT_XBC + N_DIR * SSD_HEADS
SPLIT_CQ = SPLIT_DT + Q_LORA
SPLIT_CKV = SPLIT_CQ + KV_LORA

LANES = 128
SUBLANES = 8
HEAD_PAD = 128
MLA_PAD = MLA_HEADS * HEAD_PAD
FF_CHUNK = 256
SMALL_W = LANES
DT_LANE0 = 2 * QK_ROPE
W_ALL = SSD_WIDTH + CONV_CH + Q_LORA + KV_LORA + SMALL_W
VMEM_LIMIT = 56 * 1024 * 1024


def _dot(a, b):
    return jnp.dot(a, b, preferred_element_type=F32)


def _dot_nt(a, b):
    return lax.dot_general(a, b, (((1,), (1,)), ((), ())), preferred_element_type=F32)


def _dot_tn(a, b):
    return lax.dot_general(a, b, (((0,), (0,)), ((), ())), preferred_element_type=F32)


def _silu(x):
    return x * (1.0 / (1.0 + jnp.exp(-x)))


def _rms(x, g):
    return x * lax.rsqrt(jnp.mean(x * x, axis=-1, keepdims=True) + EPS) * g


def _const_spec(shape):
    zeros = (0,) * len(shape)
    return pl.BlockSpec(shape, lambda *_: zeros, pipeline_mode=pl.Buffered(1))


def _params(n_axes, flags=None):
    return pltpu.CompilerParams(dimension_semantics=("arbitrary",) * n_axes,
                                vmem_limit_bytes=VMEM_LIMIT, flags=flags)


def _ada_kernel(c_ref, w_ref, b_ref, o_ref):
    s = _silu(c_ref[...]).astype(BF16)
    o_ref[...] = _dot(s, w_ref[...].astype(BF16)) + b_ref[...]


def _ada(c_rows, w, b):
    rows, d = c_rows.shape
    n = w.shape[1]
    tn = 1024
    return pl.pallas_call(
        _ada_kernel,
        out_shape=jax.ShapeDtypeStruct((rows, n), F32),
        grid=(n // tn,),
        in_specs=[pl.BlockSpec((rows, d), lambda j: (0, 0)),
                  pl.BlockSpec((d, tn), lambda j: (0, j)),
                  pl.BlockSpec((1, tn), lambda j: (0, j))],
        out_specs=pl.BlockSpec((rows, tn), lambda j: (0, j)),
        compiler_params=_params(1),
        name="ada",
    )(c_rows, w, b)


def _ffn_kernel(x_ref, m_ref, g_ref, wi_ref, wo_ref, *rest, k, final):
    x = x_ref[...]
    m = m_ref[0]
    if final:
        yf_ref, yb_ref, z_ref, a_ref, sn_ref, ws_ref, wm_ref, fg_ref, o_ref, acc_ref = rest
        y = (yf_ref[...] + yb_ref[...]) * _silu(z_ref[...])
        ssd = _rms(y, sn_ref[...]).astype(BF16)
        x = x + m[k - 1:k] * (_dot(ssd, ws_ref[...]) + _dot(a_ref[...], wm_ref[...]))
    else:
        o_ref, acc_ref = rest
    shift, scale, gate = m[k:k + 1], m[k + 1:k + 2], m[k + 2:k + 3]
    hn = (_rms(x, g_ref[...]) * (1 + scale) + shift).astype(BF16)
    for j in range(D_FF // FF_CHUNK):
        lo = j * FF_CHUNK
        g = _dot(hn, wi_ref[:, lo:lo + FF_CHUNK])
        u = _dot(hn, wi_ref[:, D_FF + lo:D_FF + lo + FF_CHUNK])
        a = (_silu(g) * u).astype(BF16)
        part = _dot(a, wo_ref[lo:lo + FF_CHUNK, :])
        if j == 0:
            acc_ref[...] = part
        else:
            acc_ref[...] += part
    y = x + 0.5 * gate * acc_ref[...]
    if final:
        y = _rms(y, fg_ref[...])
    o_ref[...] = y


def _ffn(x2d, m_all, m_index, g, wi, wo, k, tm, mix=None):
    n, d = x2d.shape
    final = mix is not None
    tok = lambda c: pl.BlockSpec((tm, c), lambda i: (i, 0))
    in_specs = [tok(d),
                pl.BlockSpec((1, N_MOD, d), lambda i: (m_index(i), 0, 0)),
                _const_spec((1, d)),
                _const_spec(wi.shape),
                _const_spec(wo.shape)]
    args = [x2d, m_all, g, wi, wo]
    if final:
        in_specs += [tok(a.shape[1]) for a in mix[:4]] + [_const_spec(a.shape) for a in mix[4:]]
        args += list(mix)
    return pl.pallas_call(
        functools.partial(_ffn_kernel, k=k, final=final),
        out_shape=jax.ShapeDtypeStruct((n, d), F32),
        grid=(n // tm,),
        in_specs=in_specs,
        out_specs=pl.BlockSpec((tm, d), lambda i: (i, 0)),
        scratch_shapes=[pltpu.VMEM((tm, d), F32)],
        compiler_params=_params(1),
        name="ffn_final" if final else "ffn",
    )(*args)


def _in_proj_kernel(x_ref, xp_ref, xn_ref, m_ref, g_ref, w_ref, cw_ref, cb_ref, kvn_ref, wuk_ref,
                    wuv_ref, e_ref, tk_ref, *rest, with_q, n_alias):
    if with_q:
        qn_ref, wqa_ref, wqb_ref, tqc_ref, tqs_ref = rest[:5]
        rest = rest[5:]
    rest = rest[n_alias:]
    u_ref, small_ref, k_ref, v_ref = rest[:4]
    if with_q:
        z_ref, q_ref = rest[4:6]
    ext_ref = rest[-1]
    m = m_ref[0]
    tm = x_ref.shape[1]
    i = pl.program_id(1)
    x_ext = jnp.concatenate([x_ref[0], xp_ref[0], xn_ref[0]], axis=0)
    hn_ext = (_rms(x_ext, g_ref[...]) * (1 + m[4:5]) + m[3:4]).astype(BF16)
    hn = hn_ext[0:tm]
    o_xbc = SSD_WIDTH
    o_cq = o_xbc + CONV_CH
    o_ckv = o_cq + Q_LORA
    o_small = o_ckv + KV_LORA
    xbc = _dot(hn_ext, w_ref[:, o_xbc:o_cq])
    has_prev = (i > 0).astype(F32)
    has_next = (i < pl.num_programs(1) - 1).astype(F32)
    ext_ref[0:SUBLANES] = xbc[tm:tm + SUBLANES] * has_prev
    ext_ref[SUBLANES:SUBLANES + tm] = xbc[0:tm]
    ext_ref[SUBLANES + tm:] = xbc[tm + SUBLANES:] * has_next
    half = SSD_CONV // 2
    n_conv_chunks = 4
    cc = CONV_CH // n_conv_chunks

    def conv_chunk(j):
        cols = slice(j * cc, (j + 1) * cc)
        conv = cb_ref[:, cols]
        for tap in range(SSD_CONV):
            conv = conv + ext_ref[pl.ds(SUBLANES - half + tap, tm), cols] * cw_ref[tap:tap + 1, cols]
        u_ref[0, :, cols] = _silu(conv).astype(BF16)

    small = _dot(hn, w_ref[:, o_small:o_small + SMALL_W])
    small_ref[0] = small
    ckv = _rms(_dot(hn, w_ref[:, o_ckv:o_small]), kvn_ref[...]).astype(BF16)
    conv_chunk(0)
    t = small * tk_ref[...]
    kpe = (t + pltpu.roll(t, LANES - QK_ROPE, axis=1)).astype(BF16)
    k_ref[0] = (_dot(ckv, wuk_ref[...]) + _dot(kpe, e_ref[...])).astype(BF16)
    v_ref[0] = _dot(ckv, wuv_ref[...]).T.astype(BF16)
    conv_chunk(1)
    if with_q:
        z_ref[0] = _dot(hn, w_ref[:, 0:o_xbc])
        cq = _rms(_dot(hn, w_ref[:, o_cq:o_ckv]), qn_ref[...]).astype(BF16)
        conv_chunk(2)
        tqc = jnp.concatenate([tqc_ref[...]] * MLA_HEADS, axis=1)
        tqs = jnp.concatenate([tqs_ref[...]] * MLA_HEADS, axis=1)
        qa = _dot(cq, wqa_ref[...]) * tqc
        conv_chunk(3)
        q_ref[0] = (qa + _dot(cq, wqb_ref[...]) * tqs).astype(BF16)
    else:
        conv_chunk(2)
        conv_chunk(3)


def _in_proj(h3d, m_all, m_index, g, w_all, cw, cb, kvn, wuk, wuv, e_mat, tk, tm, row0, s_total,
             q_args=None, alias_bufs=()):
    b, rows, d = h3d.shape
    with_q = q_args is not None
    blk0 = row0 // tm
    hb, n8 = tm // SUBLANES, rows // SUBLANES
    tok = lambda c: pl.BlockSpec((1, tm, c), lambda bi, i: (bi, i, 0))
    comb = lambda c: pl.BlockSpec((1, tm, c), lambda bi, i: (bi, blk0 + i, 0))
    in_specs = [tok(d),
                pl.BlockSpec((1, SUBLANES, d), lambda bi, i: (bi, jnp.maximum(i * hb - 1, 0), 0)),
                pl.BlockSpec((1, SUBLANES, d), lambda bi, i: (bi, jnp.minimum(i * hb + hb, n8 - 1), 0)),
                pl.BlockSpec((1, N_MOD, d), lambda bi, i: (m_index(bi), 0, 0)),
                _const_spec(g.shape), _const_spec(w_all.shape), _const_spec(cw.shape), _const_spec(cb.shape),
                _const_spec(kvn.shape), _const_spec(wuk.shape), _const_spec(wuv.shape),
                _const_spec(e_mat.shape), pl.BlockSpec((tm, LANES), lambda bi, i: (i, 0))]
    args = [h3d, h3d, h3d, m_all, g, w_all, cw, cb, kvn, wuk, wuv, e_mat, tk]
    out_shape = [jax.ShapeDtypeStruct((b, s_total, CONV_CH), BF16),
                 jax.ShapeDtypeStruct((b, s_total, SMALL_W), F32),
                 jax.ShapeDtypeStruct((b, s_total, MLA_PAD), BF16),
                 jax.ShapeDtypeStruct((b, MLA_HEADS * V_DIM, s_total), BF16)]
    out_specs = [comb(CONV_CH), comb(SMALL_W), comb(MLA_PAD),
                 pl.BlockSpec((1, MLA_HEADS * V_DIM, tm), lambda bi, i: (bi, 0, blk0 + i))]
    if with_q:
        qn, wqa, wqb, tqc, tqs = q_args
        in_specs += [_const_spec(qn.shape), _const_spec(wqa.shape), _const_spec(wqb.shape),
                     pl.BlockSpec((tm, LANES), lambda bi, i: (i, 0)),
                     pl.BlockSpec((tm, LANES), lambda bi, i: (i, 0))]
        args += [qn, wqa, wqb, tqc, tqs]
        out_shape += [jax.ShapeDtypeStruct((b, rows, SSD_WIDTH), F32),
                      jax.ShapeDtypeStruct((b, rows, MLA_PAD), BF16)]
        out_specs += [tok(SSD_WIDTH), tok(MLA_PAD)]
    aliases = {len(args) + j: j for j in range(len(alias_bufs))}
    in_specs += [pl.BlockSpec(memory_space=pl.ANY)] * len(alias_bufs)
    args += list(alias_bufs)
    return pl.pallas_call(
        functools.partial(_in_proj_kernel, with_q=with_q, n_alias=len(alias_bufs)),
        out_shape=out_shape,
        grid=(b, rows // tm),
        in_specs=in_specs,
        out_specs=out_specs,
        input_output_aliases=aliases,
        scratch_shapes=[pltpu.VMEM((tm + 2 * SUBLANES, CONV_CH), F32)],
        compiler_params=_params(2),
        name="in_proj_q" if with_q else "in_proj_ctx",
    )(*args)


def _ssd_kernel(uf_ref, sf_ref, ub_ref, sb_ref, dtb_ref, aneg_ref, dskip_ref, ex_ref, yf_ref, yb_ref,
                hf_ref, hb_ref, *, ncc):
    q = SSD_CHUNK
    t = pl.program_id(1)
    dirs = (0, 1)
    u_refs, small_refs, h_refs = (uf_ref, ub_ref), (sf_ref, sb_ref), (hf_ref, hb_ref)

    @pl.when(t == 0)
    def _():
        hf_ref[...] = jnp.zeros_like(hf_ref)
        hb_ref[...] = jnp.zeros_like(hb_ref)

    o_b = SSD_WIDTH
    o_c = SSD_WIDTH + SSD_GROUPS * SSD_STATE
    xs = [u_refs[d][0, :, 0:o_b].astype(F32) for d in dirs]
    bm = [u_refs[d][0, :, o_b:o_c] for d in dirs]
    cm = [u_refs[d][0, :, o_c:CONV_CH] for d in dirs]

    ri = lax.broadcasted_iota(jnp.int32, (q, q), 0)
    ci = lax.broadcasted_iota(jnp.int32, (q, q), 1)
    keep = [ri >= ci, ri <= ci]
    last = [q - 1, 0]
    dtv, parts = [], []
    for d in dirs:
        pre = small_refs[d][0] + dtb_ref[...]
        dtv.append(jnp.maximum(pre, 0.0) + jnp.log(1.0 + jnp.exp(-jnp.abs(pre))))
        a = dtv[d] * aneg_ref[...]
        a1 = a.astype(BF16)
        r1 = a - a1.astype(F32)
        a2 = r1.astype(BF16)
        parts.append((a1, a2, (r1 - a2.astype(F32)).astype(BF16)))
    tri = [jnp.where(keep[d], 1.0, 0.0).astype(BF16) for d in dirs]
    acum = [None, None]
    for j in range(3):
        for d in dirs:
            term = _dot(tri[d], parts[d][j])
            acum[d] = term if acum[d] is None else acum[d] + term
    acum_t = [acum[d].T for d in dirs]
    dend = [jnp.exp(acum[d][last[d]:last[d] + 1, :] - acum[d]) for d in dirs]
    dstart = [jnp.exp(acum[d]) for d in dirs]

    head_lane = lambda d, h: DT_LANE0 + SSD_HEADS * d + h

    def expand(arr, d):
        hi = arr.astype(BF16)
        lo = (arr - hi.astype(F32)).astype(BF16)
        return _dot(hi, ex_ref[d]) + _dot(lo, ex_ref[d])

    dstart_x = [expand(dstart[d], d) for d in dirs]
    xdt = [xs[d] * expand(dtv[d], d) for d in dirs]
    xend = [(xdt[d] * expand(dend[d], d)).astype(BF16) for d in dirs]
    gw = SSD_WIDTH // SSD_GROUPS
    hpg = SSD_HEADS // SSD_GROUPS
    glane = lax.broadcasted_iota(jnp.int32, (q, gw), 1)
    ys = [[], []]
    for g in range(SSD_GROUPS):
        gs = slice(g * SSD_STATE, (g + 1) * SSD_STATE)
        gc = slice(g * gw, (g + 1) * gw)
        cbm = [_dot_nt(cm[d][:, gs], bm[d][:, gs]) for d in dirs]
        yg = [None, None]
        for hh in range(hpg):
            in_head = jnp.logical_and(glane >= hh * SSD_HEAD_DIM, glane < (hh + 1) * SSD_HEAD_DIM)
            for d in dirs:
                hl = head_lane(d, g * hpg + hh)
                seg = acum[d][:, hl:hl + 1] - acum_t[d][hl:hl + 1, :]
                lmat = (cbm[d] * jnp.exp(jnp.where(keep[d], seg, -jnp.inf))).astype(BF16)
                part = _dot(lmat, jnp.where(in_head, xdt[d][:, gc], 0.0).astype(BF16))
                yg[d] = part if yg[d] is None else yg[d] + part
        for d in dirs:
            hst = h_refs[d][g]
            yoff = _dot(cm[d][:, gs], hst.astype(BF16)) * dstart_x[d][:, gc]
            ys[d].append(yg[d] + yoff)
            h_refs[d][g] = (dstart_x[d][last[d]:last[d] + 1, gc] * hst
                            + _dot_tn(bm[d][:, gs], xend[d][:, gc]))

    @pl.when(t >= ncc)
    def _():
        yf_ref[0] = jnp.concatenate(ys[0], axis=1) + dskip_ref[...] * xs[0]
        yb_ref[0] = jnp.concatenate(ys[1], axis=1)


def _ssd(u_all, small_all, dtb, aneg, dskip, l_x, l_c):
    b = u_all.shape[0]
    q = SSD_CHUNK
    ncx, ncc = l_x // q, l_c // q
    fwd = lambda t: jnp.where(t < ncc, ncx + t, t - ncc)
    bwd = lambda t: ncx + ncc - 1 - t
    chunk_spec = lambda w, order: pl.BlockSpec((1, q, w), lambda bi, t: (bi, order(t), 0))
    y_spec = lambda order: pl.BlockSpec((1, q, SSD_WIDTH), lambda bi, t: (bi, order(jnp.maximum(t, ncc)), 0))
    y_shape = jax.ShapeDtypeStruct((b, l_x, SSD_WIDTH), F32)
    h_buf = pltpu.VMEM((SSD_GROUPS, SSD_STATE, SSD_WIDTH // SSD_GROUPS), F32)
    ex_np = np.zeros((N_DIR, SMALL_W, SSD_WIDTH), np.float32)
    for dd in range(N_DIR):
        for h in range(SSD_HEADS):
            ex_np[dd, DT_LANE0 + SSD_HEADS * dd + h, h * SSD_HEAD_DIM:(h + 1) * SSD_HEAD_DIM] = 1.0
    ex = jnp.asarray(ex_np, BF16)
    return pl.pallas_call(
        functools.partial(_ssd_kernel, ncc=ncc),
        out_shape=[y_shape, y_shape],
        grid=(b, ncx + ncc),
        in_specs=[chunk_spec(CONV_CH, fwd), chunk_spec(SMALL_W, fwd),
                  chunk_spec(CONV_CH, bwd), chunk_spec(SMALL_W, bwd),
                  _const_spec(dtb.shape), _const_spec(aneg.shape), _const_spec(dskip.shape),
                  _const_spec(ex.shape)],
        out_specs=[y_spec(fwd), y_spec(bwd)],
        scratch_shapes=[h_buf, h_buf],
        compiler_params=_params(2),
        name="ssd",
    )(u_all, small_all, u_all, small_all, dtb, aneg, dskip, ex)


HEADS_PER_STEP = LANES // V_DIM


def _attn_kernel(q_ref, k_ref, vt_ref, o_ref):
    outs = []
    for hh in range(HEADS_PER_STEP):
        q = q_ref[0, :, hh * HEAD_PAD:(hh + 1) * HEAD_PAD]
        k = k_ref[0, :, hh * HEAD_PAD:(hh + 1) * HEAD_PAD]
        s = _dot_nt(k, q)
        m = jnp.max(s, axis=0, keepdims=True)
        p = jnp.exp2(s - m)
        l = jnp.sum(p, axis=0, keepdims=True)
        ot = _dot(vt_ref[0, hh * V_DIM:(hh + 1) * V_DIM, :], p.astype(BF16))
        outs.append((ot / l).T)
    o_ref[0] = jnp.concatenate(outs, axis=1).astype(BF16)


def _attn(q, k_all, vt_all, tq):
    b, l_x, _ = q.shape
    s_total = k_all.shape[1]
    hw = HEADS_PER_STEP * HEAD_PAD
    return pl.pallas_call(
        _attn_kernel,
        out_shape=jax.ShapeDtypeStruct((b, l_x, MLA_HEADS * V_DIM), BF16),
        grid=(b, MLA_HEADS // HEADS_PER_STEP, l_x // tq),
        in_specs=[pl.BlockSpec((1, tq, hw), lambda bi, h, i: (bi, i, h)),
                  pl.BlockSpec((1, s_total, hw), lambda bi, h, i: (bi, 0, h)),
                  pl.BlockSpec((1, LANES, s_total), lambda bi, h, i: (bi, h, 0))],
        out_specs=pl.BlockSpec((1, tq, LANES), lambda bi, h, i: (bi, i, h)),
        compiler_params=_params(3),
        name="attn",
    )(q, k_all, vt_all)


def _rope_swap(w):
    parts = []
    for base in (0, QK_ROPE // 2):
        u1 = w[..., base:base + QK_ROPE // 4]
        u2 = w[..., base + QK_ROPE // 4:base + QK_ROPE // 2]
        parts += [-u2, u1]
    return jnp.concatenate(parts, axis=-1)


def _rope_tables(l_x):
    half = QK_ROPE // 2
    inv = (ROPE_BASE ** (-np.arange(0, half, 2, dtype=np.float32) / half)).astype(np.float32)
    rows = l_x // GRID_W
    row = np.repeat(np.arange(rows), GRID_W).astype(np.float32)
    col = (np.arange(rows * GRID_W) % GRID_W).astype(np.float32)
    ar = (row[:, None] * inv).astype(np.float64)
    ac = (col[:, None] * inv).astype(np.float64)
    cos = np.concatenate([np.cos(ar), np.cos(ar), np.cos(ac), np.cos(ac)], axis=1).astype(np.float32)
    sin = np.concatenate([np.sin(ar), np.sin(ar), np.sin(ac), np.sin(ac)], axis=1).astype(np.float32)
    return cos, sin


def _pick_tile(n, pref):
    t = min(pref, n)
    while n % t:
        t //= 2
    return t


def kernel(x, c, ctx, c_ctx, w_ada, b_ada, norm_ffn1, w_ffn1_in, w_ffn1_out, norm_mix, w_in, conv_w, conv_b, dt_bias, a_log, d_skip, ssd_norm, q_norm, w_uq, kv_norm, w_ukv, w_out, norm_ffn2, w_ffn2_in, w_ffn2_out, final_norm):
    b, l_x, d = x.shape
    l_c = ctx.shape[1]
    assert d == D_MODEL and w_ada.shape[0] == 1
    assert l_x % SSD_CHUNK == 0 and l_c % SSD_CHUNK == 0 and l_x % GRID_W == 0
    s_total = l_x + l_c

    c_rows = jnp.concatenate([c, c_ctx[None, :]], axis=0)
    pad = (-c_rows.shape[0]) % SUBLANES
    c_rows = jnp.pad(c_rows, ((0, pad), (0, 0)))
    m_all = _ada(c_rows, w_ada[0], b_ada)[:b + 1].reshape(b + 1, N_MOD, d)

    tm = _pick_tile(l_x, 512)
    tmc = _pick_tile(b * l_c, 512)
    x_tiles = l_x // tm
    m_lat = lambda i: i // x_tiles
    m_ctx = lambda i: b

    wi1, wo1 = w_ffn1_in[0].astype(BF16), w_ffn1_out[0].astype(BF16)
    g1 = norm_ffn1
    x1 = _ffn(x.reshape(b * l_x, d), m_all, m_lat, g1, wi1, wo1, 0, tm)
    c1 = _ffn(ctx.reshape(b * l_c, d), m_all, m_ctx, g1, wi1, wo1, 0, tmc)

    w = w_in[0]
    w_kpe = w[:, SPLIT_CKV:]
    small = jnp.concatenate([w_kpe, _rope_swap(w_kpe), w[:, SPLIT_XBC:SPLIT_DT],
                             jnp.zeros((d, SMALL_W - 2 * QK_ROPE - N_DIR * SSD_HEADS), F32)], axis=1)
    w_all = jnp.concatenate([w[:, :SPLIT_XBC], w[:, SPLIT_DT:SPLIT_CKV], small], axis=1).astype(BF16)
    hq = QK_NOPE + QK_ROPE
    wq = w_uq[0].reshape(Q_LORA, MLA_HEADS, hq)
    zq = jnp.zeros((Q_LORA, MLA_HEADS, HEAD_PAD - hq), F32)
    wqa = jnp.concatenate([wq, zq], axis=2).reshape(Q_LORA, MLA_PAD).astype(BF16)
    wqb = jnp.concatenate([jnp.zeros((Q_LORA, MLA_HEADS, QK_NOPE), F32), _rope_swap(wq[:, :, QK_NOPE:]), zq],
                          axis=2).reshape(Q_LORA, MLA_PAD).astype(BF16)
    wkv = w_ukv[0].reshape(KV_LORA, MLA_HEADS, QK_NOPE + V_DIM)
    wuk = jnp.concatenate([wkv[:, :, :QK_NOPE], jnp.zeros((KV_LORA, MLA_HEADS, HEAD_PAD - QK_NOPE), F32)],
                          axis=2).reshape(KV_LORA, MLA_PAD).astype(BF16)
    wuv = wkv[:, :, QK_NOPE:].reshape(KV_LORA, MLA_HEADS * V_DIM).astype(BF16)
    e_np = np.zeros((SMALL_W, MLA_HEADS, HEAD_PAD), np.float32)
    for i in range(QK_ROPE):
        e_np[i, :, QK_NOPE + i] = 1.0
    e_mat = jnp.asarray(e_np.reshape(SMALL_W, MLA_PAD), BF16)

    cos, sin = _rope_tables(l_x)
    zpad = lambda n: np.zeros((l_x, n), np.float32)
    q_scale = np.float32(MLA_SCALE * math.log2(math.e))
    tqc = jnp.asarray(q_scale * np.concatenate([np.ones((l_x, QK_NOPE), np.float32), cos, zpad(HEAD_PAD - hq)], axis=1))
    tqs = jnp.asarray(q_scale * np.concatenate([zpad(QK_NOPE), sin, zpad(HEAD_PAD - hq)], axis=1))
    tk_lat = jnp.asarray(np.concatenate([cos, sin, zpad(SMALL_W - 2 * QK_ROPE)], axis=1))
    tk_ctx = jnp.asarray(np.concatenate([np.ones((l_c, QK_ROPE), np.float32),
                                         np.zeros((l_c, SMALL_W - QK_ROPE), np.float32)], axis=1))

    cw = jnp.pad(conv_w[0], ((0, SUBLANES - SSD_CONV), (0, 0)))
    shared = (norm_mix, w_all, cw, conv_b, kv_norm, wuk, wuv, e_mat)
    tmi = _pick_tile(l_x, 512)
    u_all, small_all, k_all, v_all, z, q = _in_proj(
        x1.reshape(b, l_x, d), m_all, lambda bi: bi, *shared, tk_lat, tmi, 0, s_total,
        q_args=(q_norm, wqa, wqb, tqc, tqs))
    tmic = _pick_tile(math.gcd(l_c, l_x), 256)
    u_all, small_all, k_all, v_all = _in_proj(
        c1.reshape(b, l_c, d), m_all, lambda bi: b, *shared, tk_ctx, tmic, l_x, s_total,
        alias_bufs=(u_all, small_all, k_all, v_all))

    lane_pad = lambda v: jnp.pad(v.reshape(1, -1), ((0, 0), (DT_LANE0, SMALL_W - DT_LANE0 - v.size)))
    dtb = lane_pad(dt_bias[0])
    aneg = lane_pad(-jnp.exp(a_log[0].astype(F32)))
    dskip = jnp.repeat(d_skip[0], SSD_HEAD_DIM)[None, :]
    yf, yb = _ssd(u_all, small_all, dtb, aneg, dskip, l_x, l_c)

    mla = _attn(q, k_all, v_all, _pick_tile(l_x, 512))

    wo = w_out[0]
    ws = wo[:SSD_WIDTH].astype(BF16)
    wm = wo[SSD_WIDTH:].astype(BF16)
    n = b * l_x
    mix = (yf.reshape(n, SSD_WIDTH), yb.reshape(n, SSD_WIDTH), z.reshape(n, SSD_WIDTH),
           mla.reshape(n, MLA_HEADS * V_DIM), ssd_norm, ws, wm, final_norm[None, :])
    out = _ffn(x1, m_all, m_lat, norm_ffn2, w_ffn2_in[0].astype(BF16), w_ffn2_out[0].astype(BF16), 6, tm,
               mix=mix)
    return out.reshape(b, l_x, d)
```

```python
import functools
import math

import jax
import jax.numpy as jnp
import numpy as np
from jax import lax
from jax.experimental import pallas as pl
from jax.experimental.pallas import tpu as pltpu

F32 = jnp.float32
BF16 = jnp.bfloat16

D_MODEL = 1024
GRID_W = 64
N_MOD = 9
D_FF = 2816
EPS = 1e-6
SSD_HEADS = 8
SSD_HEAD_DIM = 64
SSD_WIDTH = SSD_HEADS * SSD_HEAD_DIM
SSD_GROUPS = 2
SSD_STATE = 128
SSD_CONV = 5
SSD_CHUNK = 128
N_DIR = 2
CONV_CH = SSD_WIDTH + 2 * SSD_GROUPS * SSD_STATE
MLA_HEADS = 8
QK_NOPE = 64
QK_ROPE = 32
V_DIM = 64
Q_LORA = 768
KV_LORA = 256
MLA_SCALE = (QK_NOPE + QK_ROPE) ** -0.5
ROPE_BASE = 10000.0
SPLIT_Z = SSD_WIDTH
SPLIT_XBC = SPLIT_Z + CONV_CH
SPLIT_DT = SPLIT_XBC + N_DIR * SSD_HEADS
SPLIT_CQ = SPLIT_DT + Q_LORA
SPLIT_CKV = SPLIT_CQ + KV_LORA

LANES = 128
SUBLANES = 8
HEAD_PAD = 128
MLA_PAD = MLA_HEADS * HEAD_PAD
FF_CHUNK = 256
SMALL_W = LANES
DT_LANE0 = 2 * QK_ROPE
W_ALL = SSD_WIDTH + CONV_CH + Q_LORA + KV_LORA + SMALL_W
VMEM_LIMIT = 56 * 1024 * 1024


def _dot(a, b):
    return jnp.dot(a, b, preferred_element_type=F32)


def _dot_nt(a, b):
    return lax.dot_general(a, b, (((1,), (1,)), ((), ())), preferred_element_type=F32)


def _dot_tn(a, b):
    return lax.dot_general(a, b, (((0,), (0,)), ((), ())), preferred_element_type=F32)


def _silu(x):
    return x * (1.0 / (1.0 + jnp.exp(-x)))


def _rms(x, g):
    return x * lax.rsqrt(jnp.mean(x * x, axis=-1, keepdims=True) + EPS) * g


def _const_spec(shape):
    zeros = (0,) * len(shape)
    return pl.BlockSpec(shape, lambda *_: zeros, pipeline_mode=pl.Buffered(1))


def _params(n_axes, flags=None):
    return pltpu.CompilerParams(dimension_semantics=("arbitrary",) * n_axes,
                                vmem_limit_bytes=VMEM_LIMIT, flags=flags)


def _ada_kernel(c_ref, w_ref, b_ref, o_ref):
    s = _silu(c_ref[...]).astype(BF16)
    o_ref[...] = _dot(s, w_ref[...].astype(BF16)) + b_ref[...]


def _ada(c_rows, w, b):
    rows, d = c_rows.shape
    n = w.shape[1]
    tn = 1024
    return pl.pallas_call(
        _ada_kernel,
        out_shape=jax.ShapeDtypeStruct((rows, n), F32),
        grid=(n // tn,),
        in_specs=[pl.BlockSpec((rows, d), lambda j: (0, 0)),
                  pl.BlockSpec((d, tn), lambda j: (0, j)),
                  pl.BlockSpec((1, tn), lambda j: (0, j))],
        out_specs=pl.BlockSpec((rows, tn), lambda j: (0, j)),
        compiler_params=_params(1),
        name="ada",
    )(c_rows, w, b)


def _ffn_kernel(x_ref, m_ref, g_ref, wi_ref, wo_ref, *rest, k, final):
    x = x_ref[...]
    m = m_ref[0]
    if final:
        yf_ref, yb_ref, z_ref, a_ref, sn_ref, ws_ref, wm_ref, fg_ref, o_ref, acc_ref = rest
        y = (yf_ref[...] + yb_ref[...]) * _silu(z_ref[...])
        ssd = _rms(y, sn_ref[...]).astype(BF16)
        x = x + m[k - 1:k] * (_dot(ssd, ws_ref[...]) + _dot(a_ref[...], wm_ref[...]))
    else:
        o_ref, acc_ref = rest
    shift, scale, gate = m[k:k + 1], m[k + 1:k + 2], m[k + 2:k + 3]
    hn = (_rms(x, g_ref[...]) * (1 + scale) + shift).astype(BF16)
    for j in range(D_FF // FF_CHUNK):
        lo = j * FF_CHUNK
        g = _dot(hn, wi_ref[:, lo:lo + FF_CHUNK])
        u = _dot(hn, wi_ref[:, D_FF + lo:D_FF + lo + FF_CHUNK])
        a = (_silu(g) * u).astype(BF16)
        part = _dot(a, wo_ref[lo:lo + FF_CHUNK, :])
        if j == 0:
            acc_ref[...] = part
        else:
            acc_ref[...] += part
    y = x + 0.5 * gate * acc_ref[...]
    if final:
        y = _rms(y, fg_ref[...])
    o_ref[...] = y


def _ffn(x2d, m_all, m_index, g, wi, wo, k, tm, mix=None):
    n, d = x2d.shape
    final = mix is not None
    tok = lambda c: pl.BlockSpec((tm, c), lambda i: (i, 0))
    in_specs = [tok(d),
                pl.BlockSpec((1, N_MOD, d), lambda i: (m_index(i), 0, 0)),
                _const_spec((1, d)),
                _const_spec(wi.shape),
                _const_spec(wo.shape)]
    args = [x2d, m_all, g, wi, wo]
    if final:
        in_specs += [tok(a.shape[1]) for a in mix[:4]] + [_const_spec(a.shape) for a in mix[4:]]
        args += list(mix)
    return pl.pallas_call(
        functools.partial(_ffn_kernel, k=k, final=final),
        out_shape=jax.ShapeDtypeStruct((n, d), F32),
        grid=(n // tm,),
        in_specs=in_specs,
        out_specs=pl.BlockSpec((tm, d), lambda i: (i, 0)),
        scratch_shapes=[pltpu.VMEM((tm, d), F32)],
        compiler_params=_params(1),
        name="ffn_final" if final else "ffn",
    )(*args)


def _in_proj_kernel(x_ref, xp_ref, xn_ref, m_ref, g_ref, w_ref, cw_ref, cb_ref, kvn_ref, wuk_ref,
                    wuv_ref, e_ref, tk_ref, *rest, with_q, n_alias):
    if with_q:
        qn_ref, wqa_ref, wqb_ref, tqc_ref, tqs_ref = rest[:5]
        rest = rest[5:]
    rest = rest[n_alias:]
    u_ref, small_ref, k_ref, v_ref = rest[:4]
    if with_q:
        z_ref, q_ref = rest[4:6]
    ext_ref = rest[-1]
    m = m_ref[0]
    tm = x_ref.shape[1]
    i = pl.program_id(1)
    x_ext = jnp.concatenate([x_ref[0], xp_ref[0], xn_ref[0]], axis=0)
    hn_ext = (_rms(x_ext, g_ref[...]) * (1 + m[4:5]) + m[3:4]).astype(BF16)
    hn = hn_ext[0:tm]
    o_xbc = SSD_WIDTH
    o_cq = o_xbc + CONV_CH
    o_ckv = o_cq + Q_LORA
    o_small = o_ckv + KV_LORA
    xbc = _dot(hn_ext, w_ref[:, o_xbc:o_cq])
    has_prev = (i > 0).astype(F32)
    has_next = (i < pl.num_programs(1) - 1).astype(F32)
    ext_ref[0:SUBLANES] = xbc[tm:tm + SUBLANES] * has_prev
    ext_ref[SUBLANES:SUBLANES + tm] = xbc[0:tm]
    ext_ref[SUBLANES + tm:] = xbc[tm + SUBLANES:] * has_next
    half = SSD_CONV // 2
    n_conv_chunks = 4
    cc = CONV_CH // n_conv_chunks

    def conv_chunk(j):
        cols = slice(j * cc, (j + 1) * cc)
        conv = cb_ref[:, cols]
        for tap in range(SSD_CONV):
            conv = conv + ext_ref[pl.ds(SUBLANES - half + tap, tm), cols] * cw_ref[tap:tap + 1, cols]
        u_ref[0, :, cols] = _silu(conv).astype(BF16)

    small = _dot(hn, w_ref[:, o_small:o_small + SMALL_W])
    small_ref[0] = small
    ckv = _rms(_dot(hn, w_ref[:, o_ckv:o_small]), kvn_ref[...]).astype(BF16)
    conv_chunk(0)
    t = small * tk_ref[...]
    kpe = (t + pltpu.roll(t, LANES - QK_ROPE, axis=1)).astype(BF16)
    k_ref[0] = (_dot(ckv, wuk_ref[...]) + _dot(kpe, e_ref[...])).astype(BF16)
    vt = _dot(ckv, wuv_ref[...]).T.astype(BF16)
    for n in range(tm // KEY_BLOCK):
        v_ref[0, n] = vt[:, n * KEY_BLOCK:(n + 1) * KEY_BLOCK]
    conv_chunk(1)
    if with_q:
        z_ref[0] = _dot(hn, w_ref[:, 0:o_xbc])
        cq = _rms(_dot(hn, w_ref[:, o_cq:o_ckv]), qn_ref[...]).astype(BF16)
        conv_chunk(2)
        tqc = jnp.concatenate([tqc_ref[...]] * MLA_HEADS, axis=1)
        tqs = jnp.concatenate([tqs_ref[...]] * MLA_HEADS, axis=1)
        qa = _dot(cq, wqa_ref[...]) * tqc
        conv_chunk(3)
        q_ref[0] = (qa + _dot(cq, wqb_ref[...]) * tqs).astype(BF16)
    else:
        conv_chunk(2)
        conv_chunk(3)


def _in_proj(h3d, m_all, m_index, g, w_all, cw, cb, kvn, wuk, wuv, e_mat, tk, tm, row0, s_total,
             q_args=None, alias_bufs=()):
    b, rows, d = h3d.shape
    with_q = q_args is not None
    blk0 = row0 // tm
    hb, n8 = tm // SUBLANES, rows // SUBLANES
    tok = lambda c: pl.BlockSpec((1, tm, c), lambda bi, i: (bi, i, 0))
    comb = lambda c: pl.BlockSpec((1, tm, c), lambda bi, i: (bi, blk0 + i, 0))
    in_specs = [tok(d),
                pl.BlockSpec((1, SUBLANES, d), lambda bi, i: (bi, jnp.maximum(i * hb - 1, 0), 0)),
                pl.BlockSpec((1, SUBLANES, d), lambda bi, i: (bi, jnp.minimum(i * hb + hb, n8 - 1), 0)),
                pl.BlockSpec((1, N_MOD, d), lambda bi, i: (m_index(bi), 0, 0)),
                _const_spec(g.shape), _const_spec(w_all.shape), _const_spec(cw.shape), _const_spec(cb.shape),
                _const_spec(kvn.shape), _const_spec(wuk.shape), _const_spec(wuv.shape),
                _const_spec(e_mat.shape), pl.BlockSpec((tm, LANES), lambda bi, i: (i, 0))]
    args = [h3d, h3d, h3d, m_all, g, w_all, cw, cb, kvn, wuk, wuv, e_mat, tk]
    out_shape = [jax.ShapeDtypeStruct((b, s_total, CONV_CH), BF16),
                 jax.ShapeDtypeStruct((b, s_total, SMALL_W), F32),
                 jax.ShapeDtypeStruct((b, s_total, MLA_PAD), BF16),
                 jax.ShapeDtypeStruct((b, s_total // KEY_BLOCK, MLA_HEADS * V_DIM, KEY_BLOCK), BF16)]
    out_specs = [comb(CONV_CH), comb(SMALL_W), comb(MLA_PAD),
                 pl.BlockSpec((1, tm // KEY_BLOCK, MLA_HEADS * V_DIM, KEY_BLOCK),
                              lambda bi, i: (bi, blk0 + i, 0, 0))]
    if with_q:
        qn, wqa, wqb, tqc, tqs = q_args
        in_specs += [_const_spec(qn.shape), _const_spec(wqa.shape), _const_spec(wqb.shape),
                     pl.BlockSpec((tm, LANES), lambda bi, i: (i, 0)),
                     pl.BlockSpec((tm, LANES), lambda bi, i: (i, 0))]
        args += [qn, wqa, wqb, tqc, tqs]
        out_shape += [jax.ShapeDtypeStruct((b, rows, SSD_WIDTH), F32),
                      jax.ShapeDtypeStruct((b, rows, MLA_PAD), BF16)]
        out_specs += [tok(SSD_WIDTH), tok(MLA_PAD)]
    aliases = {len(args) + j: j for j in range(len(alias_bufs))}
    in_specs += [pl.BlockSpec(memory_space=pl.ANY)] * len(alias_bufs)
    args += list(alias_bufs)
    return pl.pallas_call(
        functools.partial(_in_proj_kernel, with_q=with_q, n_alias=len(alias_bufs)),
        out_shape=out_shape,
        grid=(b, rows // tm),
        in_specs=in_specs,
        out_specs=out_specs,
        input_output_aliases=aliases,
        scratch_shapes=[pltpu.VMEM((tm + 2 * SUBLANES, CONV_CH), F32)],
        compiler_params=_params(2),
        name="in_proj_q" if with_q else "in_proj_ctx",
    )(*args)


def _ssd_kernel(uf_ref, sf_ref, ub_ref, sb_ref, dtb_ref, aneg_ref, dskip_ref, ex_ref, yf_ref, yb_ref,
                hf_ref, hb_ref, *, ncc):
    q = SSD_CHUNK
    t = pl.program_id(1)
    dirs = (0, 1)
    u_refs, small_refs, h_refs = (uf_ref, ub_ref), (sf_ref, sb_ref), (hf_ref, hb_ref)

    @pl.when(t == 0)
    def _():
        hf_ref[...] = jnp.zeros_like(hf_ref)
        hb_ref[...] = jnp.zeros_like(hb_ref)

    o_b = SSD_WIDTH
    o_c = SSD_WIDTH + SSD_GROUPS * SSD_STATE
    xs = [u_refs[d][0, :, 0:o_b].astype(F32) for d in dirs]
    bm = [u_refs[d][0, :, o_b:o_c] for d in dirs]
    cm = [u_refs[d][0, :, o_c:CONV_CH] for d in dirs]

    ri = lax.broadcasted_iota(jnp.int32, (q, q), 0)
    ci = lax.broadcasted_iota(jnp.int32, (q, q), 1)
    keep = [ri >= ci, ri <= ci]
    last = [q - 1, 0]
    dtv, parts = [], []
    for d in dirs:
        pre = small_refs[d][0] + dtb_ref[...]
        dtv.append(jnp.maximum(pre, 0.0) + jnp.log(1.0 + jnp.exp(-jnp.abs(pre))))
        a = dtv[d] * aneg_ref[...]
        a1 = a.astype(BF16)
        r1 = a - a1.astype(F32)
        a2 = r1.astype(BF16)
        parts.append((a1, a2, (r1 - a2.astype(F32)).astype(BF16)))
    tri = [jnp.where(keep[d], 1.0, 0.0).astype(BF16) for d in dirs]
    acum = [None, None]
    for j in range(3):
        for d in dirs:
            term = _dot(tri[d], parts[d][j])
            acum[d] = term if acum[d] is None else acum[d] + term
    acum_t = [acum[d].T for d in dirs]
    dend = [jnp.exp(acum[d][last[d]:last[d] + 1, :] - acum[d]) for d in dirs]
    dstart = [jnp.exp(acum[d]) for d in dirs]

    head_lane = lambda d, h: DT_LANE0 + SSD_HEADS * d + h

    def expand(arr, d):
        hi = arr.astype(BF16)
        lo = (arr - hi.astype(F32)).astype(BF16)
        return _dot(hi, ex_ref[d]) + _dot(lo, ex_ref[d])

    dstart_x = [expand(dstart[d], d) for d in dirs]
    xdt = [xs[d] * expand(dtv[d], d) for d in dirs]
    xend = [(xdt[d] * expand(dend[d], d)).astype(BF16) for d in dirs]
    gw = SSD_WIDTH // SSD_GROUPS
    hpg = SSD_HEADS // SSD_GROUPS
    glane = lax.broadcasted_iota(jnp.int32, (q, gw), 1)
    ys = [[], []]
    for g in range(SSD_GROUPS):
        gs = slice(g * SSD_STATE, (g + 1) * SSD_STATE)
        gc = slice(g * gw, (g + 1) * gw)
        cbm = [_dot_nt(cm[d][:, gs], bm[d][:, gs]) for d in dirs]
        yg = [None, None]
        for hh in range(hpg):
            in_head = jnp.logical_and(glane >= hh * SSD_HEAD_DIM, glane < (hh + 1) * SSD_HEAD_DIM)
            for d in dirs:
                hl = head_lane(d, g * hpg + hh)
                seg = acum[d][:, hl:hl + 1] - acum_t[d][hl:hl + 1, :]
                lmat = (cbm[d] * jnp.exp(jnp.where(keep[d], seg, -jnp.inf))).astype(BF16)
                part = _dot(lmat, jnp.where(in_head, xdt[d][:, gc], 0.0).astype(BF16))
                yg[d] = part if yg[d] is None else yg[d] + part
        for d in dirs:
            hst = h_refs[d][g]
            yoff = _dot(cm[d][:, gs], hst.astype(BF16)) * dstart_x[d][:, gc]
            ys[d].append(yg[d] + yoff)
            h_refs[d][g] = (dstart_x[d][last[d]:last[d] + 1, gc] * hst
                            + _dot_tn(bm[d][:, gs], xend[d][:, gc]))

    @pl.when(t >= ncc)
    def _():
        yf_ref[0] = jnp.concatenate(ys[0], axis=1) + dskip_ref[...] * xs[0]
        yb_ref[0] = jnp.concatenate(ys[1], axis=1)


def _ssd(u_all, small_all, dtb, aneg, dskip, l_x, l_c):
    b = u_all.shape[0]
    q = SSD_CHUNK
    ncx, ncc = l_x // q, l_c // q
    fwd = lambda t: jnp.where(t < ncc, ncx + t, t - ncc)
    bwd = lambda t: ncx + ncc - 1 - t
    chunk_spec = lambda w, order: pl.BlockSpec((1, q, w), lambda bi, t: (bi, order(t), 0))
    y_spec = lambda order: pl.BlockSpec((1, q, SSD_WIDTH), lambda bi, t: (bi, order(jnp.maximum(t, ncc)), 0))
    y_shape = jax.ShapeDtypeStruct((b, l_x, SSD_WIDTH), F32)
    h_buf = pltpu.VMEM((SSD_GROUPS, SSD_STATE, SSD_WIDTH // SSD_GROUPS), F32)
    ex_np = np.zeros((N_DIR, SMALL_W, SSD_WIDTH), np.float32)
    for dd in range(N_DIR):
        for h in range(SSD_HEADS):
            ex_np[dd, DT_LANE0 + SSD_HEADS * dd + h, h * SSD_HEAD_DIM:(h + 1) * SSD_HEAD_DIM] = 1.0
    ex = jnp.asarray(ex_np, BF16)
    return pl.pallas_call(
        functools.partial(_ssd_kernel, ncc=ncc),
        out_shape=[y_shape, y_shape],
        grid=(b, ncx + ncc),
        in_specs=[chunk_spec(CONV_CH, fwd), chunk_spec(SMALL_W, fwd),
                  chunk_spec(CONV_CH, bwd), chunk_spec(SMALL_W, bwd),
                  _const_spec(dtb.shape), _const_spec(aneg.shape), _const_spec(dskip.shape),
                  _const_spec(ex.shape)],
        out_specs=[y_spec(fwd), y_spec(bwd)],
        scratch_shapes=[h_buf, h_buf],
        compiler_params=_params(2),
        name="ssd",
    )(u_all, small_all, u_all, small_all, dtb, aneg, dskip, ex)


HEADS_PER_STEP = LANES // V_DIM


KEY_BLOCK = 256


def _attn_kernel(q_ref, k_ref, vt_ref, o_ref):
    nb = vt_ref.shape[1]
    kb = vt_ref.shape[3]
    heads = range(HEADS_PER_STEP)
    ones = jnp.ones((2 * SUBLANES, kb), BF16)
    q = [q_ref[0, :, hh * HEAD_PAD:(hh + 1) * HEAD_PAD] for hh in heads]

    def scores(j):
        return [_dot_nt(k_ref[0, j * kb:(j + 1) * kb, hh * HEAD_PAD:(hh + 1) * HEAD_PAD], q[hh])
                for hh in heads]

    m = [None] * HEADS_PER_STEP
    acc = [None] * HEADS_PER_STEP
    s_next = scores(0)
    for j in range(nb):
        s = s_next
        if j + 1 < nb:
            s_next = scores(j + 1)
        for hh in heads:
            vt = jnp.concatenate([vt_ref[0, j, hh * V_DIM:(hh + 1) * V_DIM, :], ones], axis=0)
            m_blk = jnp.max(s[hh], axis=0, keepdims=True)
            m_new = m_blk if j == 0 else jnp.maximum(m[hh], m_blk)
            part = _dot(vt, jnp.exp2(s[hh] - m_new).astype(BF16))
            acc[hh] = part if j == 0 else acc[hh] * jnp.exp2(m[hh] - m_new) + part
            m[hh] = m_new

    outs = [(acc[hh][0:V_DIM] / acc[hh][V_DIM:V_DIM + 1]).T for hh in heads]
    o_ref[0] = jnp.concatenate(outs, axis=1).astype(BF16)


def _attn(q, k_all, vt_all, tq):
    b, l_x, _ = q.shape
    s_total = k_all.shape[1]
    nb, kb = vt_all.shape[1], vt_all.shape[3]
    hw = HEADS_PER_STEP * HEAD_PAD
    return pl.pallas_call(
        _attn_kernel,
        out_shape=jax.ShapeDtypeStruct((b, l_x, MLA_HEADS * V_DIM), BF16),
        grid=(b, MLA_HEADS // HEADS_PER_STEP, l_x // tq),
        in_specs=[pl.BlockSpec((1, tq, hw), lambda bi, h, i: (bi, i, h)),
                  pl.BlockSpec((1, s_total, hw), lambda bi, h, i: (bi, 0, h)),
                  pl.BlockSpec((1, nb, LANES, kb), lambda bi, h, i: (bi, 0, h, 0))],
        out_specs=pl.BlockSpec((1, tq, LANES), lambda bi, h, i: (bi, i, h)),
        compiler_params=_params(3),
        name="attn",
    )(q, k_all, vt_all)


def _rope_swap(w):
    parts = []
    for base in (0, QK_ROPE // 2):
        u1 = w[..., base:base + QK_ROPE // 4]
        u2 = w[..., base + QK_ROPE // 4:base + QK_ROPE // 2]
        parts += [-u2, u1]
    return jnp.concatenate(parts, axis=-1)


def _rope_tables(l_x):
    half = QK_ROPE // 2
    inv = (ROPE_BASE ** (-np.arange(0, half, 2, dtype=np.float32) / half)).astype(np.float32)
    rows = l_x // GRID_W
    row = np.repeat(np.arange(rows), GRID_W).astype(np.float32)
    col = (np.arange(rows * GRID_W) % GRID_W).astype(np.float32)
    ar = (row[:, None] * inv).astype(np.float64)
    ac = (col[:, None] * inv).astype(np.float64)
    cos = np.concatenate([np.cos(ar), np.cos(ar), np.cos(ac), np.cos(ac)], axis=1).astype(np.float32)
    sin = np.concatenate([np.sin(ar), np.sin(ar), np.sin(ac), np.sin(ac)], axis=1).astype(np.float32)
    return cos, sin


def _pick_tile(n, pref):
    t = min(pref, n)
    while n % t:
        t //= 2
    return t


def kernel(x, c, ctx, c_ctx, w_ada, b_ada, norm_ffn1, w_ffn1_in, w_ffn1_out, norm_mix, w_in, conv_w, conv_b, dt_bias, a_log, d_skip, ssd_norm, q_norm, w_uq, kv_norm, w_ukv, w_out, norm_ffn2, w_ffn2_in, w_ffn2_out, final_norm):
    b, l_x, d = x.shape
    l_c = ctx.shape[1]
    assert d == D_MODEL and w_ada.shape[0] == 1
    assert l_x % SSD_CHUNK == 0 and l_c % SSD_CHUNK == 0 and l_x % GRID_W == 0
    s_total = l_x + l_c

    c_rows = jnp.concatenate([c, c_ctx[None, :]], axis=0)
    pad = (-c_rows.shape[0]) % SUBLANES
    c_rows = jnp.pad(c_rows, ((0, pad), (0, 0)))
    m_all = _ada(c_rows, w_ada[0], b_ada)[:b + 1].reshape(b + 1, N_MOD, d)

    tm = _pick_tile(l_x, 512)
    tmc = _pick_tile(b * l_c, 512)
    x_tiles = l_x // tm
    m_lat = lambda i: i // x_tiles
    m_ctx = lambda i: b

    wi1, wo1 = w_ffn1_in[0].astype(BF16), w_ffn1_out[0].astype(BF16)
    g1 = norm_ffn1
    x1 = _ffn(x.reshape(b * l_x, d), m_all, m_lat, g1, wi1, wo1, 0, tm)
    c1 = _ffn(ctx.reshape(b * l_c, d), m_all, m_ctx, g1, wi1, wo1, 0, tmc)

    w = w_in[0]
    w_kpe = w[:, SPLIT_CKV:]
    small = jnp.concatenate([w_kpe, _rope_swap(w_kpe), w[:, SPLIT_XBC:SPLIT_DT],
                             jnp.zeros((d, SMALL_W - 2 * QK_ROPE - N_DIR * SSD_HEADS), F32)], axis=1)
    w_all = jnp.concatenate([w[:, :SPLIT_XBC], w[:, SPLIT_DT:SPLIT_CKV], small], axis=1).astype(BF16)
    hq = QK_NOPE + QK_ROPE
    wq = w_uq[0].reshape(Q_LORA, MLA_HEADS, hq)
    zq = jnp.zeros((Q_LORA, MLA_HEADS, HEAD_PAD - hq), F32)
    wqa = jnp.concatenate([wq, zq], axis=2).reshape(Q_LORA, MLA_PAD).astype(BF16)
    wqb = jnp.concatenate([jnp.zeros((Q_LORA, MLA_HEADS, QK_NOPE), F32), _rope_swap(wq[:, :, QK_NOPE:]), zq],
                          axis=2).reshape(Q_LORA, MLA_PAD).astype(BF16)
    wkv = w_ukv[0].reshape(KV_LORA, MLA_HEADS, QK_NOPE + V_DIM)
    wuk = jnp.concatenate([wkv[:, :, :QK_NOPE], jnp.zeros((KV_LORA, MLA_HEADS, HEAD_PAD - QK_NOPE), F32)],
                          axis=2).reshape(KV_LORA, MLA_PAD).astype(BF16)
    wuv = wkv[:, :, QK_NOPE:].reshape(KV_LORA, MLA_HEADS * V_DIM).astype(BF16)
    e_np = np.zeros((SMALL_W, MLA_HEADS, HEAD_PAD), np.float32)
    for i in range(QK_ROPE):
        e_np[i, :, QK_NOPE + i] = 1.0
    e_mat = jnp.asarray(e_np.reshape(SMALL_W, MLA_PAD), BF16)

    cos, sin = _rope_tables(l_x)
    zpad = lambda n: np.zeros((l_x, n), np.float32)
    q_scale = np.float32(MLA_SCALE * math.log2(math.e))
    tqc = jnp.asarray(q_scale * np.concatenate([np.ones((l_x, QK_NOPE), np.float32), cos, zpad(HEAD_PAD - hq)], axis=1))
    tqs = jnp.asarray(q_scale * np.concatenate([zpad(QK_NOPE), sin, zpad(HEAD_PAD - hq)], axis=1))
    tk_lat = jnp.asarray(np.concatenate([cos, sin, zpad(SMALL_W - 2 * QK_ROPE)], axis=1))
    tk_ctx = jnp.asarray(np.concatenate([np.ones((l_c, QK_ROPE), np.float32),
                                         np.zeros((l_c, SMALL_W - QK_ROPE), np.float32)], axis=1))

    cw = jnp.pad(conv_w[0], ((0, SUBLANES - SSD_CONV), (0, 0)))
    shared = (norm_mix, w_all, cw, conv_b, kv_norm, wuk, wuv, e_mat)
    tmi = _pick_tile(l_x, 512)
    u_all, small_all, k_all, v_all, z, q = _in_proj(
        x1.reshape(b, l_x, d), m_all, lambda bi: bi, *shared, tk_lat, tmi, 0, s_total,
        q_args=(q_norm, wqa, wqb, tqc, tqs))
    tmic = _pick_tile(math.gcd(l_c, l_x), 256)
    u_all, small_all, k_all, v_all = _in_proj(
        c1.reshape(b, l_c, d), m_all, lambda bi: b, *shared, tk_ctx, tmic, l_x, s_total,
        alias_bufs=(u_all, small_all, k_all, v_all))

    lane_pad = lambda v: jnp.pad(v.reshape(1, -1), ((0, 0), (DT_LANE0, SMALL_W - DT_LANE0 - v.size)))
    dtb = lane_pad(dt_bias[0])
    aneg = lane_pad(-jnp.exp(a_log[0].astype(F32)))
    dskip = jnp.repeat(d_skip[0], SSD_HEAD_DIM)[None, :]
    yf, yb = _ssd(u_all, small_all, dtb, aneg, dskip, l_x, l_c)

    mla = _attn(q, k_all, v_all, _pick_tile(l_x, 1024))

    wo = w_out[0]
    ws = wo[:SSD_WIDTH].astype(BF16)
    wm = wo[SSD_WIDTH:].astype(BF16)
    n = b * l_x
    mix = (yf.reshape(n, SSD_WIDTH), yb.reshape(n, SSD_WIDTH), z.reshape(n, SSD_WIDTH),
           mla.reshape(n, MLA_HEADS * V_DIM), ssd_norm, ws, wm, final_norm[None, :])
    out = _ffn(x1, m_all, m_lat, norm_ffn2, w_ffn2_in[0].astype(BF16), w_ffn2_out[0].astype(BF16), 6, tm,
               mix=mix)
    return out.reshape(b, l_x, d)
```

```python
import functools
import math

import jax
import jax.numpy as jnp
import numpy as np
from jax import lax
from jax.experimental import pallas as pl
from jax.experimental.pallas import tpu as pltpu

F32 = jnp.float32
BF16 = jnp.bfloat16

D_MODEL = 1024
GRID_W = 64
N_MOD = 9
D_FF = 2816
EPS = 1e-6
SSD_HEADS = 8
SSD_HEAD_DIM = 64
SSD_WIDTH = SSD_HEADS * SSD_HEAD_DIM
SSD_GROUPS = 2
SSD_STATE = 128
SSD_CONV = 5
SSD_CHUNK = 128
N_DIR = 2
CONV_CH = SSD_WIDTH + 2 * SSD_GROUPS * SSD_STATE
MLA_HEADS = 8
QK_NOPE = 64
QK_ROPE = 32
V_DIM = 64
Q_LORA = 768
KV_LORA = 256
MLA_SCALE = (QK_NOPE + QK_ROPE) ** -0.5
ROPE_BASE = 10000.0
SPLIT_Z = SSD_WIDTH
SPLIT_XBC = SPLIT_Z + CONV_CH
SPLIT_DT = SPLIT_XBC + N_DIR * SSD_HEADS
SPLIT_CQ = SPLIT_DT + Q_LORA
SPLIT_CKV = SPLIT_CQ + KV_LORA

LANES = 128
SUBLANES = 8
HEAD_PAD = 128
MLA_PAD = MLA_HEADS * HEAD_PAD
FF_CHUNK = 256
SMALL_W = LANES
DT_LANE0 = 2 * QK_ROPE
W_ALL = SSD_WIDTH + CONV_CH + Q_LORA + KV_LORA + SMALL_W
VMEM_LIMIT = 56 * 1024 * 1024


def _dot(a, b):
    return jnp.dot(a, b, preferred_element_type=F32)


def _dot_nt(a, b):
    return lax.dot_general(a, b, (((1,), (1,)), ((), ())), preferred_element_type=F32)


def _dot_tn(a, b):
    return lax.dot_general(a, b, (((0,), (0,)), ((), ())), preferred_element_type=F32)


def _silu(x):
    return x * (1.0 / (1.0 + jnp.exp(-x)))


def _rms(x, g):
    return x * lax.rsqrt(jnp.mean(x * x, axis=-1, keepdims=True) + EPS) * g


def _const_spec(shape):
    zeros = (0,) * len(shape)
    return pl.BlockSpec(shape, lambda *_: zeros, pipeline_mode=pl.Buffered(1))


def _params(n_axes, flags=None):
    return pltpu.CompilerParams(dimension_semantics=("arbitrary",) * n_axes,
                                vmem_limit_bytes=VMEM_LIMIT, flags=flags)


def _ada_kernel(c_ref, w_ref, b_ref, o_ref):
    s = _silu(c_ref[...]).astype(BF16)
    o_ref[...] = _dot(s, w_ref[...].astype(BF16)) + b_ref[...]


def _ada(c_rows, w, b):
    rows, d = c_rows.shape
    n = w.shape[1]
    tn = 1024
    return pl.pallas_call(
        _ada_kernel,
        out_shape=jax.ShapeDtypeStruct((rows, n), F32),
        grid=(n // tn,),
        in_specs=[pl.BlockSpec((rows, d), lambda j: (0, 0)),
                  pl.BlockSpec((d, tn), lambda j: (0, j)),
                  pl.BlockSpec((1, tn), lambda j: (0, j))],
        out_specs=pl.BlockSpec((rows, tn), lambda j: (0, j)),
        compiler_params=_params(1),
        name="ada",
    )(c_rows, w, b)


def _ffn_kernel(x_ref, m_ref, g_ref, wi_ref, wo_ref, *rest, k, final):
    x = x_ref[...]
    m = m_ref[0]
    if final:
        yf_ref, yb_ref, z_ref, a_ref, sn_ref, ws_ref, wm_ref, fg_ref, o_ref, acc_ref = rest
        y = (yf_ref[...] + yb_ref[...]) * _silu(z_ref[...])
        ssd = _rms(y, sn_ref[...]).astype(BF16)
        x = x + m[k - 1:k] * (_dot(ssd, ws_ref[...]) + _dot(a_ref[...], wm_ref[...]))
    else:
        o_ref, acc_ref = rest
    shift, scale, gate = m[k:k + 1], m[k + 1:k + 2], m[k + 2:k + 3]
    hn = (_rms(x, g_ref[...]) * (1 + scale) + shift).astype(BF16)
    for j in range(D_FF // FF_CHUNK):
        lo = j * FF_CHUNK
        g = _dot(hn, wi_ref[:, lo:lo + FF_CHUNK])
        u = _dot(hn, wi_ref[:, D_FF + lo:D_FF + lo + FF_CHUNK])
        a = (_silu(g) * u).astype(BF16)
        part = _dot(a, wo_ref[lo:lo + FF_CHUNK, :])
        if j == 0:
            acc_ref[...] = part
        else:
            acc_ref[...] += part
    y = x + 0.5 * gate * acc_ref[...]
    if final:
        y = _rms(y, fg_ref[...])
    o_ref[...] = y


def _ffn(x2d, m_all, m_index, g, wi, wo, k, tm, mix=None):
    n, d = x2d.shape
    final = mix is not None
    tok = lambda c: pl.BlockSpec((tm, c), lambda i: (i, 0))
    in_specs = [tok(d),
                pl.BlockSpec((1, N_MOD, d), lambda i: (m_index(i), 0, 0)),
                _const_spec((1, d)),
                _const_spec(wi.shape),
                _const_spec(wo.shape)]
    args = [x2d, m_all, g, wi, wo]
    if final:
        in_specs += [tok(a.shape[1]) for a in mix[:4]] + [_const_spec(a.shape) for a in mix[4:]]
        args += list(mix)
    return pl.pallas_call(
        functools.partial(_ffn_kernel, k=k, final=final),
        out_shape=jax.ShapeDtypeStruct((n, d), F32),
        grid=(n // tm,),
        in_specs=in_specs,
        out_specs=pl.BlockSpec((tm, d), lambda i: (i, 0)),
        scratch_shapes=[pltpu.VMEM((tm, d), F32)],
        compiler_params=_params(1),
        name="ffn_final" if final else "ffn",
    )(*args)


def _in_proj_kernel(x_ref, xp_ref, xn_ref, m_ref, g_ref, w_ref, cw_ref, cb_ref, kvn_ref, wuk_ref,
                    wuv_ref, e_ref, tk_ref, *rest, with_q, n_alias):
    if with_q:
        qn_ref, wqa_ref, tqc_ref, tqs_ref = rest[:4]
        rest = rest[4:]
    rest = rest[n_alias:]
    u_ref, small_ref, k_ref, v_ref = rest[:4]
    if with_q:
        z_ref, q_ref = rest[4:6]
    ext_ref = rest[-1]
    m = m_ref[0]
    tm = x_ref.shape[1]
    i = pl.program_id(1)
    x_ext = jnp.concatenate([x_ref[0], xp_ref[0], xn_ref[0]], axis=0)
    hn_ext = (_rms(x_ext, g_ref[...]) * (1 + m[4:5]) + m[3:4]).astype(BF16)
    hn = hn_ext[0:tm]
    o_xbc = SSD_WIDTH
    o_cq = o_xbc + CONV_CH
    o_ckv = o_cq + Q_LORA
    o_small = o_ckv + KV_LORA
    xbc = _dot(hn_ext, w_ref[:, o_xbc:o_cq])
    has_prev = (i > 0).astype(F32)
    has_next = (i < pl.num_programs(1) - 1).astype(F32)
    ext_ref[0:SUBLANES] = xbc[tm:tm + SUBLANES] * has_prev
    ext_ref[SUBLANES:SUBLANES + tm] = xbc[0:tm]
    ext_ref[SUBLANES + tm:] = xbc[tm + SUBLANES:] * has_next
    half = SSD_CONV // 2
    n_conv_chunks = 4
    cc = CONV_CH // n_conv_chunks

    def conv_chunk(j):
        cols = slice(j * cc, (j + 1) * cc)
        conv = cb_ref[:, cols]
        for tap in range(SSD_CONV):
            conv = conv + ext_ref[pl.ds(SUBLANES - half + tap, tm), cols] * cw_ref[tap:tap + 1, cols]
        u_ref[0, :, cols] = _silu(conv).astype(BF16)

    small = _dot(hn, w_ref[:, o_small:o_small + SMALL_W])
    small_ref[0] = small
    ckv = _rms(_dot(hn, w_ref[:, o_ckv:o_small]), kvn_ref[...]).astype(BF16)
    conv_chunk(0)
    t = small * tk_ref[...]
    kpe = (t + pltpu.roll(t, LANES - QK_ROPE, axis=1)).astype(BF16)
    k_ref[0] = (_dot(ckv, wuk_ref[...]) + _dot(kpe, e_ref[...])).astype(BF16)
    vt = _dot(ckv, wuv_ref[...]).T.astype(BF16)
    for n in range(tm // KEY_BLOCK):
        v_ref[0, n] = vt[:, n * KEY_BLOCK:(n + 1) * KEY_BLOCK]
    conv_chunk(1)
    if with_q:
        z_ref[0] = _dot(hn, w_ref[:, 0:o_xbc])
        cq = _rms(_dot(hn, w_ref[:, o_cq:o_ckv]), qn_ref[...]).astype(BF16)
        conv_chunk(2)
        qa = _dot(cq, wqa_ref[...])
        conv_chunk(3)
        tqc, tqs = tqc_ref[...], tqs_ref[...]
        first_half = lax.broadcasted_iota(jnp.int32, (tm, HEAD_PAD), 1) % (QK_ROPE // 2) < QK_ROPE // 4
        for h in range(MLA_HEADS):
            hs = slice(h * HEAD_PAD, (h + 1) * HEAD_PAD)
            qh = qa[:, hs]
            partner = jnp.where(first_half, pltpu.roll(qh, HEAD_PAD - QK_ROPE // 4, axis=1),
                                pltpu.roll(qh, QK_ROPE // 4, axis=1))
            q_ref[0, :, hs] = (qh * tqc + partner * tqs).astype(BF16)
    else:
        conv_chunk(2)
        conv_chunk(3)


def _in_proj(h3d, m_all, m_index, g, w_all, cw, cb, kvn, wuk, wuv, e_mat, tk, tm, row0, s_total,
             q_args=None, alias_bufs=()):
    b, rows, d = h3d.shape
    with_q = q_args is not None
    blk0 = row0 // tm
    hb, n8 = tm // SUBLANES, rows // SUBLANES
    tok = lambda c: pl.BlockSpec((1, tm, c), lambda bi, i: (bi, i, 0))
    comb = lambda c: pl.BlockSpec((1, tm, c), lambda bi, i: (bi, blk0 + i, 0))
    in_specs = [tok(d),
                pl.BlockSpec((1, SUBLANES, d), lambda bi, i: (bi, jnp.maximum(i * hb - 1, 0), 0)),
                pl.BlockSpec((1, SUBLANES, d), lambda bi, i: (bi, jnp.minimum(i * hb + hb, n8 - 1), 0)),
                pl.BlockSpec((1, N_MOD, d), lambda bi, i: (m_index(bi), 0, 0)),
                _const_spec(g.shape), _const_spec(w_all.shape), _const_spec(cw.shape), _const_spec(cb.shape),
                _const_spec(kvn.shape), _const_spec(wuk.shape), _const_spec(wuv.shape),
                _const_spec(e_mat.shape), pl.BlockSpec((tm, LANES), lambda bi, i: (i, 0))]
    args = [h3d, h3d, h3d, m_all, g, w_all, cw, cb, kvn, wuk, wuv, e_mat, tk]
    out_shape = [jax.ShapeDtypeStruct((b, s_total, CONV_CH), BF16),
                 jax.ShapeDtypeStruct((b, s_total, SMALL_W), F32),
                 jax.ShapeDtypeStruct((b, s_total, MLA_PAD), BF16),
                 jax.ShapeDtypeStruct((b, s_total // KEY_BLOCK, MLA_HEADS * V_DIM, KEY_BLOCK), BF16)]
    out_specs = [comb(CONV_CH), comb(SMALL_W), comb(MLA_PAD),
                 pl.BlockSpec((1, tm // KEY_BLOCK, MLA_HEADS * V_DIM, KEY_BLOCK),
                              lambda bi, i: (bi, blk0 + i, 0, 0))]
    if with_q:
        qn, wqa, tqc, tqs = q_args
        in_specs += [_const_spec(qn.shape), _const_spec(wqa.shape),
                     pl.BlockSpec((tm, LANES), lambda bi, i: (i, 0)),
                     pl.BlockSpec((tm, LANES), lambda bi, i: (i, 0))]
        args += [qn, wqa, tqc, tqs]
        out_shape += [jax.ShapeDtypeStruct((b, rows, SSD_WIDTH), F32),
                      jax.ShapeDtypeStruct((b, rows, MLA_PAD), BF16)]
        out_specs += [tok(SSD_WIDTH), tok(MLA_PAD)]
    aliases = {len(args) + j: j for j in range(len(alias_bufs))}
    in_specs += [pl.BlockSpec(memory_space=pl.ANY)] * len(alias_bufs)
    args += list(alias_bufs)
    return pl.pallas_call(
        functools.partial(_in_proj_kernel, with_q=with_q, n_alias=len(alias_bufs)),
        out_shape=out_shape,
        grid=(b, rows // tm),
        in_specs=in_specs,
        out_specs=out_specs,
        input_output_aliases=aliases,
        scratch_shapes=[pltpu.VMEM((tm + 2 * SUBLANES, CONV_CH), F32)],
        compiler_params=_params(2),
        name="in_proj_q" if with_q else "in_proj_ctx",
    )(*args)


def _ssd_kernel(uf_ref, sf_ref, ub_ref, sb_ref, dtb_ref, aneg_ref, dskip_ref, ex_ref, yf_ref, yb_ref,
                hf_ref, hb_ref, *, ncc):
    q = SSD_CHUNK
    t = pl.program_id(1)
    dirs = (0, 1)
    u_refs, small_refs, h_refs = (uf_ref, ub_ref), (sf_ref, sb_ref), (hf_ref, hb_ref)

    @pl.when(t == 0)
    def _():
        hf_ref[...] = jnp.zeros_like(hf_ref)
        hb_ref[...] = jnp.zeros_like(hb_ref)

    o_b = SSD_WIDTH
    o_c = SSD_WIDTH + SSD_GROUPS * SSD_STATE
    xs = [u_refs[d][0, :, 0:o_b].astype(F32) for d in dirs]
    bm = [u_refs[d][0, :, o_b:o_c] for d in dirs]
    cm = [u_refs[d][0, :, o_c:CONV_CH] for d in dirs]

    ri = lax.broadcasted_iota(jnp.int32, (q, q), 0)
    ci = lax.broadcasted_iota(jnp.int32, (q, q), 1)
    keep = [ri >= ci, ri <= ci]
    last = [q - 1, 0]
    dtv, parts = [], []
    for d in dirs:
        pre = small_refs[d][0] + dtb_ref[...]
        dtv.append(jnp.maximum(pre, 0.0) + jnp.log(1.0 + jnp.exp(-jnp.abs(pre))))
        a = dtv[d] * aneg_ref[...]
        a1 = a.astype(BF16)
        r1 = a - a1.astype(F32)
        a2 = r1.astype(BF16)
        a3 = (r1 - a2.astype(F32)).astype(BF16)
        parts.append(jnp.concatenate([a1, a2, a3], axis=0))
    tri = [jnp.where(keep[d], 1.0, 0.0).astype(BF16) for d in dirs]
    acum = [_dot(jnp.concatenate([tri[d]] * 3, axis=1), parts[d]) for d in dirs]
    acum_t = [acum[d].T for d in dirs]
    dend = [jnp.exp(acum[d][last[d]:last[d] + 1, :] - acum[d]) for d in dirs]
    dstart = [jnp.exp(acum[d]) for d in dirs]

    head_lane = lambda d, h: DT_LANE0 + SSD_HEADS * d + h

    def expand(arr, d):
        hi = arr.astype(BF16)
        lo = (arr - hi.astype(F32)).astype(BF16)
        return _dot(jnp.concatenate([hi, lo], axis=1), ex_ref[d])

    dstart_x = [expand(dstart[d], d) for d in dirs]
    xdt = [xs[d] * expand(dtv[d], d) for d in dirs]
    xend = [(xdt[d] * expand(dend[d], d)).astype(BF16) for d in dirs]
    gw = SSD_WIDTH // SSD_GROUPS
    hpg = SSD_HEADS // SSD_GROUPS
    glane = lax.broadcasted_iota(jnp.int32, (q, gw), 1)
    ys = [[], []]
    for g in range(SSD_GROUPS):
        gs = slice(g * SSD_STATE, (g + 1) * SSD_STATE)
        gc = slice(g * gw, (g + 1) * gw)
        cbm = [_dot_nt(cm[d][:, gs], bm[d][:, gs]) for d in dirs]
        yg = [None, None]
        for h0 in range(0, hpg, 2):
            for d in dirs:
                lmats, xmasks = [], []
                for hh in (h0, h0 + 1):
                    hl = head_lane(d, g * hpg + hh)
                    seg = acum[d][:, hl:hl + 1] - acum_t[d][hl:hl + 1, :]
                    lmats.append((cbm[d] * jnp.exp(jnp.where(keep[d], seg, -jnp.inf))).astype(BF16))
                    in_head = jnp.logical_and(glane >= hh * SSD_HEAD_DIM, glane < (hh + 1) * SSD_HEAD_DIM)
                    xmasks.append(jnp.where(in_head, xdt[d][:, gc], 0.0).astype(BF16))
                part = _dot(jnp.concatenate(lmats, axis=1), jnp.concatenate(xmasks, axis=0))
                yg[d] = part if yg[d] is None else yg[d] + part
        for d in dirs:
            hst = h_refs[d][g]
            yoff = _dot(cm[d][:, gs], hst.astype(BF16)) * dstart_x[d][:, gc]
            ys[d].append(yg[d] + yoff)
            h_refs[d][g] = (dstart_x[d][last[d]:last[d] + 1, gc] * hst
                            + _dot_tn(bm[d][:, gs], xend[d][:, gc]))

    @pl.when(t >= ncc)
    def _():
        yf_ref[0] = jnp.concatenate(ys[0], axis=1) + dskip_ref[...] * xs[0]
        yb_ref[0] = jnp.concatenate(ys[1], axis=1)


def _ssd(u_all, small_all, dtb, aneg, dskip, l_x, l_c):
    b = u_all.shape[0]
    q = SSD_CHUNK
    ncx, ncc = l_x // q, l_c // q
    fwd = lambda t: jnp.where(t < ncc, ncx + t, t - ncc)
    bwd = lambda t: ncx + ncc - 1 - t
    chunk_spec = lambda w, order: pl.BlockSpec((1, q, w), lambda bi, t: (bi, order(t), 0))
    y_spec = lambda order: pl.BlockSpec((1, q, SSD_WIDTH), lambda bi, t: (bi, order(jnp.maximum(t, ncc)), 0))
    y_shape = jax.ShapeDtypeStruct((b, l_x, SSD_WIDTH), F32)
    h_buf = pltpu.VMEM((SSD_GROUPS, SSD_STATE, SSD_WIDTH // SSD_GROUPS), F32)
    ex_np = np.zeros((N_DIR, SMALL_W, SSD_WIDTH), np.float32)
    for dd in range(N_DIR):
        for h in range(SSD_HEADS):
            ex_np[dd, DT_LANE0 + SSD_HEADS * dd + h, h * SSD_HEAD_DIM:(h + 1) * SSD_HEAD_DIM] = 1.0
    ex = jnp.asarray(np.concatenate([ex_np, ex_np], axis=1), BF16)
    return pl.pallas_call(
        functools.partial(_ssd_kernel, ncc=ncc),
        out_shape=[y_shape, y_shape],
        grid=(b, ncx + ncc),
        in_specs=[chunk_spec(CONV_CH, fwd), chunk_spec(SMALL_W, fwd),
                  chunk_spec(CONV_CH, bwd), chunk_spec(SMALL_W, bwd),
                  _const_spec(dtb.shape), _const_spec(aneg.shape), _const_spec(dskip.shape),
                  _const_spec(ex.shape)],
        out_specs=[y_spec(fwd), y_spec(bwd)],
        scratch_shapes=[h_buf, h_buf],
        compiler_params=_params(2),
        name="ssd",
    )(u_all, small_all, u_all, small_all, dtb, aneg, dskip, ex)


HEADS_PER_STEP = LANES // V_DIM


KEY_BLOCK = 256


def _attn_kernel(q_ref, k_ref, vt_ref, o_ref):
    nb = vt_ref.shape[1]
    kb = vt_ref.shape[3]
    heads = range(HEADS_PER_STEP)
    ones = jnp.ones((2 * SUBLANES, kb), BF16)
    q = [q_ref[0, :, hh * HEAD_PAD:(hh + 1) * HEAD_PAD] for hh in heads]

    def scores(j):
        return [_dot_nt(k_ref[0, j * kb:(j + 1) * kb, hh * HEAD_PAD:(hh + 1) * HEAD_PAD], q[hh])
                for hh in heads]

    m = [None] * HEADS_PER_STEP
    acc = [None] * HEADS_PER_STEP
    s_next = scores(0)
    for j in range(nb):
        s = s_next
        if j + 1 < nb:
            s_next = scores(j + 1)
        for hh in heads:
            vt = jnp.concatenate([vt_ref[0, j, hh * V_DIM:(hh + 1) * V_DIM, :], ones], axis=0)
            m_blk = jnp.max(s[hh], axis=0, keepdims=True)
            m_new = m_blk if j == 0 else jnp.maximum(m[hh], m_blk)
            part = _dot(vt, jnp.exp2(s[hh] - m_new).astype(BF16))
            acc[hh] = part if j == 0 else acc[hh] * jnp.exp2(m[hh] - m_new) + part
            m[hh] = m_new

    outs = [(acc[hh][0:V_DIM] / acc[hh][V_DIM:V_DIM + 1]).T for hh in heads]
    o_ref[0] = jnp.concatenate(outs, axis=1).astype(BF16)


def _attn(q, k_all, vt_all, tq):
    b, l_x, _ = q.shape
    s_total = k_all.shape[1]
    nb, kb = vt_all.shape[1], vt_all.shape[3]
    hw = HEADS_PER_STEP * HEAD_PAD
    return pl.pallas_call(
        _attn_kernel,
        out_shape=jax.ShapeDtypeStruct((b, l_x, MLA_HEADS * V_DIM), BF16),
        grid=(b, MLA_HEADS // HEADS_PER_STEP, l_x // tq),
        in_specs=[pl.BlockSpec((1, tq, hw), lambda bi, h, i: (bi, i, h)),
                  pl.BlockSpec((1, s_total, hw), lambda bi, h, i: (bi, 0, h)),
                  pl.BlockSpec((1, nb, LANES, kb), lambda bi, h, i: (bi, 0, h, 0))],
        out_specs=pl.BlockSpec((1, tq, LANES), lambda bi, h, i: (bi, i, h)),
        compiler_params=_params(3),
        name="attn",
    )(q, k_all, vt_all)


def _rope_swap(w):
    parts = []
    for base in (0, QK_ROPE // 2):
        u1 = w[..., base:base + QK_ROPE // 4]
        u2 = w[..., base + QK_ROPE // 4:base + QK_ROPE // 2]
        parts += [-u2, u1]
    return jnp.concatenate(parts, axis=-1)


def _rope_tables(l_x):
    half = QK_ROPE // 2
    inv = (ROPE_BASE ** (-np.arange(0, half, 2, dtype=np.float32) / half)).astype(np.float32)
    rows = l_x // GRID_W
    row = np.repeat(np.arange(rows), GRID_W).astype(np.float32)
    col = (np.arange(rows * GRID_W) % GRID_W).astype(np.float32)
    ar = (row[:, None] * inv).astype(np.float64)
    ac = (col[:, None] * inv).astype(np.float64)
    cos = np.concatenate([np.cos(ar), np.cos(ar), np.cos(ac), np.cos(ac)], axis=1).astype(np.float32)
    sin = np.concatenate([np.sin(ar), np.sin(ar), np.sin(ac), np.sin(ac)], axis=1).astype(np.float32)
    return cos, sin


def _pick_tile(n, pref):
    t = min(pref, n)
    while n % t:
        t //= 2
    return t


def kernel(x, c, ctx, c_ctx, w_ada, b_ada, norm_ffn1, w_ffn1_in, w_ffn1_out, norm_mix, w_in, conv_w, conv_b, dt_bias, a_log, d_skip, ssd_norm, q_norm, w_uq, kv_norm, w_ukv, w_out, norm_ffn2, w_ffn2_in, w_ffn2_out, final_norm):
    b, l_x, d = x.shape
    l_c = ctx.shape[1]
    assert d == D_MODEL and w_ada.shape[0] == 1
    assert l_x % SSD_CHUNK == 0 and l_c % SSD_CHUNK == 0 and l_x % GRID_W == 0
    s_total = l_x + l_c

    c_rows = jnp.concatenate([c, c_ctx[None, :]], axis=0)
    pad = (-c_rows.shape[0]) % SUBLANES
    c_rows = jnp.pad(c_rows, ((0, pad), (0, 0)))
    m_all = _ada(c_rows, w_ada[0], b_ada)[:b + 1].reshape(b + 1, N_MOD, d)

    tm = _pick_tile(l_x, 512)
    tmc = _pick_tile(b * l_c, 512)
    x_tiles = l_x // tm
    m_lat = lambda i: i // x_tiles
    m_ctx = lambda i: b

    wi1, wo1 = w_ffn1_in[0].astype(BF16), w_ffn1_out[0].astype(BF16)
    g1 = norm_ffn1
    x1 = _ffn(x.reshape(b * l_x, d), m_all, m_lat, g1, wi1, wo1, 0, tm)
    c1 = _ffn(ctx.reshape(b * l_c, d), m_all, m_ctx, g1, wi1, wo1, 0, tmc)

    w = w_in[0]
    w_kpe = w[:, SPLIT_CKV:]
    small = jnp.concatenate([w_kpe, _rope_swap(w_kpe), w[:, SPLIT_XBC:SPLIT_DT],
                             jnp.zeros((d, SMALL_W - 2 * QK_ROPE - N_DIR * SSD_HEADS), F32)], axis=1)
    w_all = jnp.concatenate([w[:, :SPLIT_XBC], w[:, SPLIT_DT:SPLIT_CKV], small], axis=1).astype(BF16)
    hq = QK_NOPE + QK_ROPE
    wq = w_uq[0].reshape(Q_LORA, MLA_HEADS, hq)
    zq = jnp.zeros((Q_LORA, MLA_HEADS, HEAD_PAD - hq), F32)
    wqa = jnp.concatenate([wq, zq], axis=2).reshape(Q_LORA, MLA_PAD).astype(BF16)
    wkv = w_ukv[0].reshape(KV_LORA, MLA_HEADS, QK_NOPE + V_DIM)
    wuk = jnp.concatenate([wkv[:, :, :QK_NOPE], jnp.zeros((KV_LORA, MLA_HEADS, HEAD_PAD - QK_NOPE), F32)],
                          axis=2).reshape(KV_LORA, MLA_PAD).astype(BF16)
    wuv = wkv[:, :, QK_NOPE:].reshape(KV_LORA, MLA_HEADS * V_DIM).astype(BF16)
    e_np = np.zeros((SMALL_W, MLA_HEADS, HEAD_PAD), np.float32)
    for i in range(QK_ROPE):
        e_np[i, :, QK_NOPE + i] = 1.0
    e_mat = jnp.asarray(e_np.reshape(SMALL_W, MLA_PAD), BF16)

    cos, sin = _rope_tables(l_x)
    zpad = lambda n: np.zeros((l_x, n), np.float32)
    q_scale = np.float32(MLA_SCALE * math.log2(math.e))
    tqc = jnp.asarray(q_scale * np.concatenate([np.ones((l_x, QK_NOPE), np.float32), cos, zpad(HEAD_PAD - hq)], axis=1))
    sign = np.tile(np.repeat(np.float32([-1.0, 1.0]), QK_ROPE // 4), 2)
    tqs = jnp.asarray(q_scale * np.concatenate([zpad(QK_NOPE), sin * sign, zpad(HEAD_PAD - hq)], axis=1))
    tk_lat = jnp.asarray(np.concatenate([cos, sin, zpad(SMALL_W - 2 * QK_ROPE)], axis=1))
    tk_ctx = jnp.asarray(np.concatenate([np.ones((l_c, QK_ROPE), np.float32),
                                         np.zeros((l_c, SMALL_W - QK_ROPE), np.float32)], axis=1))

    cw = jnp.pad(conv_w[0], ((0, SUBLANES - SSD_CONV), (0, 0)))
    shared = (norm_mix, w_all, cw, conv_b, kv_norm, wuk, wuv, e_mat)
    tmi = _pick_tile(l_x, 1024)
    u_all, small_all, k_all, v_all, z, q = _in_proj(
        x1.reshape(b, l_x, d), m_all, lambda bi: bi, *shared, tk_lat, tmi, 0, s_total,
        q_args=(q_norm, wqa, tqc, tqs))
    tmic = _pick_tile(math.gcd(l_c, l_x), 256)
    u_all, small_all, k_all, v_all = _in_proj(
        c1.reshape(b, l_c, d), m_all, lambda bi: b, *shared, tk_ctx, tmic, l_x, s_total,
        alias_bufs=(u_all, small_all, k_all, v_all))

    lane_pad = lambda v: jnp.pad(v.reshape(1, -1), ((0, 0), (DT_LANE0, SMALL_W - DT_LANE0 - v.size)))
    dtb = lane_pad(dt_bias[0])
    aneg = lane_pad(-jnp.exp(a_log[0].astype(F32)))
    dskip = jnp.repeat(d_skip[0], SSD_HEAD_DIM)[None, :]
    yf, yb = _ssd(u_all, small_all, dtb, aneg, dskip, l_x, l_c)

    mla = _attn(q, k_all, v_all, _pick_tile(l_x, 1024))

    wo = w_out[0]
    ws = wo[:SSD_WIDTH].astype(BF16)
    wm = wo[SSD_WIDTH:].astype(BF16)
    n = b * l_x
    mix = (yf.reshape(n, SSD_WIDTH), yb.reshape(n, SSD_WIDTH), z.reshape(n, SSD_WIDTH),
           mla.reshape(n, MLA_HEADS * V_DIM), ssd_norm, ws, wm, final_norm[None, :])
    out = _ffn(x1, m_all, m_lat, norm_ffn2, w_ffn2_in[0].astype(BF16), w_ffn2_out[0].astype(BF16), 6, tm,
               mix=mix)
    return out.reshape(b, l_x, d)
```

```python
import functools
import math

import jax
import jax.numpy as jnp
import numpy as np
from jax import lax
from jax.experimental import pallas as pl
from jax.experimental.pallas import tpu as pltpu

F32 = jnp.float32
BF16 = jnp.bfloat16

D_MODEL = 1024
GRID_W = 64
N_MOD = 9
D_FF = 2816
EPS = 1e-6
SSD_HEADS = 8
SSD_HEAD_DIM = 64
SSD_WIDTH = SSD_HEADS * SSD_HEAD_DIM
SSD_GROUPS = 2
SSD_STATE = 128
SSD_CONV = 5
SSD_CHUNK = 128
N_DIR = 2
CONV_CH = SSD_WIDTH + 2 * SSD_GROUPS * SSD_STATE
MLA_HEADS = 8
QK_NOPE = 64
QK_ROPE = 32
V_DIM = 64
Q_LORA = 768
KV_LORA = 256
MLA_SCALE = (QK_NOPE + QK_ROPE) ** -0.5
ROPE_BASE = 10000.0
SPLIT_Z = SSD_WIDTH
SPLIT_XBC = SPLIT_Z + CONV_CH
SPLIT_DT = SPLIT_XBC + N_DIR * SSD_HEADS
SPLIT_CQ = SPLIT_DT + Q_LORA
SPLIT_CKV = SPLIT_CQ + KV_LORA

LANES = 128
SUBLANES = 8
HEAD_PAD = 128
MLA_PAD = MLA_HEADS * HEAD_PAD
FF_CHUNK = 256
SMALL_W = LANES
DT_LANE0 = 2 * QK_ROPE
W_ALL = SSD_WIDTH + CONV_CH + Q_LORA + KV_LORA + SMALL_W
VMEM_LIMIT = 56 * 1024 * 1024


def _dot(a, b):
    return jnp.dot(a, b, preferred_element_type=F32)


def _dot_nt(a, b):
    return lax.dot_general(a, b, (((1,), (1,)), ((), ())), preferred_element_type=F32)


def _dot_tn(a, b):
    return lax.dot_general(a, b, (((0,), (0,)), ((), ())), preferred_element_type=F32)


def _silu(x):
    return x * (1.0 / (1.0 + jnp.exp(-x)))


def _rms(x, g):
    return x * lax.rsqrt(jnp.mean(x * x, axis=-1, keepdims=True) + EPS) * g


def _const_spec(shape):
    zeros = (0,) * len(shape)
    return pl.BlockSpec(shape, lambda *_: zeros, pipeline_mode=pl.Buffered(1))


def _params(n_axes, flags=None):
    return pltpu.CompilerParams(dimension_semantics=("arbitrary",) * n_axes,
                                vmem_limit_bytes=VMEM_LIMIT, flags=flags)


def _ada_kernel(c_ref, w_ref, b_ref, o_ref):
    s = _silu(c_ref[...]).astype(BF16)
    o_ref[...] = _dot(s, w_ref[...].astype(BF16)) + b_ref[...]


def _ada(c_rows, w, b):
    rows, d = c_rows.shape
    n = w.shape[1]
    tn = 1024
    return pl.pallas_call(
        _ada_kernel,
        out_shape=jax.ShapeDtypeStruct((rows, n), F32),
        grid=(n // tn,),
        in_specs=[pl.BlockSpec((rows, d), lambda j: (0, 0)),
                  pl.BlockSpec((d, tn), lambda j: (0, j)),
                  pl.BlockSpec((1, tn), lambda j: (0, j))],
        out_specs=pl.BlockSpec((rows, tn), lambda j: (0, j)),
        compiler_params=_params(1),
        name="ada",
    )(c_rows, w, b)


def _ffn_kernel(x_ref, m_ref, g_ref, wi_ref, wo_ref, *rest, k, final):
    x = x_ref[...]
    m = m_ref[0]
    if final:
        yf_ref, yb_ref, z_ref, a_ref, sn_ref, ws_ref, wm_ref, fg_ref, o_ref, acc_ref = rest
        y = (yf_ref[...] + yb_ref[...]) * _silu(z_ref[...])
        ssd = _rms(y, sn_ref[...]).astype(BF16)
        x = x + m[k - 1:k] * (_dot(ssd, ws_ref[...]) + _dot(a_ref[...], wm_ref[...]))
    else:
        o_ref, acc_ref = rest
    shift, scale, gate = m[k:k + 1], m[k + 1:k + 2], m[k + 2:k + 3]
    hn = (_rms(x, g_ref[...]) * (1 + scale) + shift).astype(BF16)
    for j in range(D_FF // FF_CHUNK):
        lo = j * FF_CHUNK
        g = _dot(hn, wi_ref[:, lo:lo + FF_CHUNK])
        u = _dot(hn, wi_ref[:, D_FF + lo:D_FF + lo + FF_CHUNK])
        a = (_silu(g) * u).astype(BF16)
        part = _dot(a, wo_ref[lo:lo + FF_CHUNK, :])
        if j == 0:
            acc_ref[...] = part
        else:
            acc_ref[...] += part
    y = x + 0.5 * gate * acc_ref[...]
    if final:
        y = _rms(y, fg_ref[...])
    o_ref[...] = y


def _ffn(x2d, m_all, m_index, g, wi, wo, k, tm, mix=None):
    n, d = x2d.shape
    final = mix is not None
    tok = lambda c: pl.BlockSpec((tm, c), lambda i: (i, 0))
    in_specs = [tok(d),
                pl.BlockSpec((1, N_MOD, d), lambda i: (m_index(i), 0, 0)),
                _const_spec((1, d)),
                _const_spec(wi.shape),
                _const_spec(wo.shape)]
    args = [x2d, m_all, g, wi, wo]
    if final:
        in_specs += [tok(a.shape[1]) for a in mix[:4]] + [_const_spec(a.shape) for a in mix[4:]]
        args += list(mix)
    return pl.pallas_call(
        functools.partial(_ffn_kernel, k=k, final=final),
        out_shape=jax.ShapeDtypeStruct((n, d), F32),
        grid=(n // tm,),
        in_specs=in_specs,
        out_specs=pl.BlockSpec((tm, d), lambda i: (i, 0)),
        scratch_shapes=[pltpu.VMEM((tm, d), F32)],
        compiler_params=_params(1),
        name="ffn_final" if final else "ffn",
    )(*args)


def _in_proj_kernel(x_ref, xp_ref, xn_ref, m_ref, g_ref, w_ref, cw_ref, cb_ref, kvn_ref, wuk_ref,
                    wuv_ref, e_ref, tk_ref, *rest, with_q, n_alias):
    if with_q:
        qn_ref, wqa_ref, tqc_ref, tqs_ref = rest[:4]
        rest = rest[4:]
    rest = rest[n_alias:]
    u_ref, small_ref, k_ref, v_ref = rest[:4]
    if with_q:
        z_ref, q_ref = rest[4:6]
    ext_ref = rest[-1]
    m = m_ref[0]
    tm = x_ref.shape[1]
    i = pl.program_id(1)
    x_ext = jnp.concatenate([x_ref[0], xp_ref[0], xn_ref[0]], axis=0)
    hn_ext = (_rms(x_ext, g_ref[...]) * (1 + m[4:5]) + m[3:4]).astype(BF16)
    hn = hn_ext[0:tm]
    o_xbc = SSD_WIDTH
    o_cq = o_xbc + CONV_CH
    o_ckv = o_cq + Q_LORA
    o_small = o_ckv + KV_LORA
    xbc = _dot(hn_ext, w_ref[:, o_xbc:o_cq])
    has_prev = (i > 0).astype(F32)
    has_next = (i < pl.num_programs(1) - 1).astype(F32)
    ext_ref[0:SUBLANES] = xbc[tm:tm + SUBLANES] * has_prev
    ext_ref[SUBLANES:SUBLANES + tm] = xbc[0:tm]
    ext_ref[SUBLANES + tm:] = xbc[tm + SUBLANES:] * has_next
    half = SSD_CONV // 2
    n_conv_chunks = 4
    cc = CONV_CH // n_conv_chunks

    def conv_chunk(j):
        cols = slice(j * cc, (j + 1) * cc)
        conv = cb_ref[:, cols]
        for tap in range(SSD_CONV):
            conv = conv + ext_ref[pl.ds(SUBLANES - half + tap, tm), cols] * cw_ref[tap:tap + 1, cols]
        u_ref[0, :, cols] = _silu(conv).astype(BF16)

    small = _dot(hn, w_ref[:, o_small:o_small + SMALL_W])
    small_ref[0] = small
    ckv = _rms(_dot(hn, w_ref[:, o_ckv:o_small]), kvn_ref[...]).astype(BF16)
    conv_chunk(0)
    t = small * tk_ref[...]
    kpe = (t + pltpu.roll(t, LANES - QK_ROPE, axis=1)).astype(BF16)
    k_ref[0] = (_dot(ckv, wuk_ref[...]) + _dot(kpe, e_ref[...])).astype(BF16)
    vt = _dot(ckv, wuv_ref[...]).T.astype(BF16)
    for n in range(tm // KEY_BLOCK):
        v_ref[0, n] = vt[:, n * KEY_BLOCK:(n + 1) * KEY_BLOCK]
    conv_chunk(1)
    if with_q:
        z_ref[0] = _dot(hn, w_ref[:, 0:o_xbc])
        cq = _rms(_dot(hn, w_ref[:, o_cq:o_ckv]), qn_ref[...]).astype(BF16)
        conv_chunk(2)
        qa = _dot(cq, wqa_ref[...])
        conv_chunk(3)
        tqc, tqs = tqc_ref[...], tqs_ref[...]
        first_half = lax.broadcasted_iota(jnp.int32, (tm, HEAD_PAD), 1) % (QK_ROPE // 2) < QK_ROPE // 4
        for h in range(MLA_HEADS):
            hs = slice(h * HEAD_PAD, (h + 1) * HEAD_PAD)
            qh = qa[:, hs]
            partner = jnp.where(first_half, pltpu.roll(qh, HEAD_PAD - QK_ROPE // 4, axis=1),
                                pltpu.roll(qh, QK_ROPE // 4, axis=1))
            q_ref[0, :, hs] = (qh * tqc + partner * tqs).astype(BF16)
    else:
        conv_chunk(2)
        conv_chunk(3)


def _in_proj(h3d, m_all, m_index, g, w_all, cw, cb, kvn, wuk, wuv, e_mat, tk, tm, row0, s_total,
             q_args=None, alias_bufs=()):
    b, rows, d = h3d.shape
    with_q = q_args is not None
    blk0 = row0 // tm
    hb, n8 = tm // SUBLANES, rows // SUBLANES
    tok = lambda c: pl.BlockSpec((1, tm, c), lambda bi, i: (bi, i, 0))
    comb = lambda c: pl.BlockSpec((1, tm, c), lambda bi, i: (bi, blk0 + i, 0))
    in_specs = [tok(d),
                pl.BlockSpec((1, SUBLANES, d), lambda bi, i: (bi, jnp.maximum(i * hb - 1, 0), 0)),
                pl.BlockSpec((1, SUBLANES, d), lambda bi, i: (bi, jnp.minimum(i * hb + hb, n8 - 1), 0)),
                pl.BlockSpec((1, N_MOD, d), lambda bi, i: (m_index(bi), 0, 0)),
                _const_spec(g.shape), _const_spec(w_all.shape), _const_spec(cw.shape), _const_spec(cb.shape),
                _const_spec(kvn.shape), _const_spec(wuk.shape), _const_spec(wuv.shape),
                _const_spec(e_mat.shape), pl.BlockSpec((tm, LANES), lambda bi, i: (i, 0))]
    args = [h3d, h3d, h3d, m_all, g, w_all, cw, cb, kvn, wuk, wuv, e_mat, tk]
    out_shape = [jax.ShapeDtypeStruct((b, s_total, CONV_CH), BF16),
                 jax.ShapeDtypeStruct((b, s_total, SMALL_W), F32),
                 jax.ShapeDtypeStruct((b, s_total, MLA_PAD), BF16),
                 jax.ShapeDtypeStruct((b, s_total // KEY_BLOCK, MLA_HEADS * V_DIM, KEY_BLOCK), BF16)]
    out_specs = [comb(CONV_CH), comb(SMALL_W), comb(MLA_PAD),
                 pl.BlockSpec((1, tm // KEY_BLOCK, MLA_HEADS * V_DIM, KEY_BLOCK),
                              lambda bi, i: (bi, blk0 + i, 0, 0))]
    if with_q:
        qn, wqa, tqc, tqs = q_args
        in_specs += [_const_spec(qn.shape), _const_spec(wqa.shape),
                     pl.BlockSpec((tm, LANES), lambda bi, i: (i, 0)),
                     pl.BlockSpec((tm, LANES), lambda bi, i: (i, 0))]
        args += [qn, wqa, tqc, tqs]
        out_shape += [jax.ShapeDtypeStruct((b, rows, SSD_WIDTH), F32),
                      jax.ShapeDtypeStruct((b, rows, MLA_PAD), BF16)]
        out_specs += [tok(SSD_WIDTH), tok(MLA_PAD)]
    aliases = {len(args) + j: j for j in range(len(alias_bufs))}
    in_specs += [pl.BlockSpec(memory_space=pl.ANY)] * len(alias_bufs)
    args += list(alias_bufs)
    return pl.pallas_call(
        functools.partial(_in_proj_kernel, with_q=with_q, n_alias=len(alias_bufs)),
        out_shape=out_shape,
        grid=(b, rows // tm),
        in_specs=in_specs,
        out_specs=out_specs,
        input_output_aliases=aliases,
        scratch_shapes=[pltpu.VMEM((tm + 2 * SUBLANES, CONV_CH), F32)],
        compiler_params=_params(2),
        name="in_proj_q" if with_q else "in_proj_ctx",
    )(*args)


def _ssd_kernel(uf_ref, sf_ref, ub_ref, sb_ref, dtb_ref, aneg_ref, dskip_ref, ex_ref, yf_ref, yb_ref,
                hf_ref, hb_ref, *, ncc):
    q = SSD_CHUNK
    t = pl.program_id(1)
    dirs = (0, 1)
    u_refs, small_refs, h_refs = (uf_ref, ub_ref), (sf_ref, sb_ref), (hf_ref, hb_ref)

    @pl.when(t == 0)
    def _():
        hf_ref[...] = jnp.zeros_like(hf_ref)
        hb_ref[...] = jnp.zeros_like(hb_ref)

    o_b = SSD_WIDTH
    o_c = SSD_WIDTH + SSD_GROUPS * SSD_STATE
    xs = [u_refs[d][0, :, 0:o_b].astype(F32) for d in dirs]
    bm = [u_refs[d][0, :, o_b:o_c] for d in dirs]
    cm = [u_refs[d][0, :, o_c:CONV_CH] for d in dirs]

    ri = lax.broadcasted_iota(jnp.int32, (q, q), 0)
    ci = lax.broadcasted_iota(jnp.int32, (q, q), 1)
    keep = [ri >= ci, ri <= ci]
    last = [q - 1, 0]
    dtv, parts = [], []
    for d in dirs:
        pre = small_refs[d][0] + dtb_ref[...]
        dtv.append(jnp.maximum(pre, 0.0) + jnp.log(1.0 + jnp.exp(-jnp.abs(pre))))
        a = dtv[d] * aneg_ref[...]
        a1 = a.astype(BF16)
        r1 = a - a1.astype(F32)
        a2 = r1.astype(BF16)
        a3 = (r1 - a2.astype(F32)).astype(BF16)
        parts.append(jnp.concatenate([a1, a2, a3], axis=0))
    tri = [jnp.where(keep[d], 1.0, 0.0).astype(BF16) for d in dirs]
    acum = [_dot(jnp.concatenate([tri[d]] * 3, axis=1), parts[d]) for d in dirs]
    acum_t = [acum[d].T for d in dirs]
    dend = [jnp.exp(acum[d][last[d]:last[d] + 1, :] - acum[d]) for d in dirs]
    dstart = [jnp.exp(acum[d]) for d in dirs]

    head_lane = lambda d, h: DT_LANE0 + SSD_HEADS * d + h

    def expand(arr, d):
        hi = arr.astype(BF16)
        lo = (arr - hi.astype(F32)).astype(BF16)
        return _dot(jnp.concatenate([hi, lo], axis=1), ex_ref[d])

    dstart_x = [expand(dstart[d], d) for d in dirs]
    xdt = [xs[d] * expand(dtv[d], d) for d in dirs]
    xend = [(xdt[d] * expand(dend[d], d)).astype(BF16) for d in dirs]
    gw = SSD_WIDTH // SSD_GROUPS
    hpg = SSD_HEADS // SSD_GROUPS
    glane = lax.broadcasted_iota(jnp.int32, (q, gw), 1)
    ys = [[], []]
    for g in range(SSD_GROUPS):
        gs = slice(g * SSD_STATE, (g + 1) * SSD_STATE)
        gc = slice(g * gw, (g + 1) * gw)
        cbm = [_dot_nt(cm[d][:, gs], bm[d][:, gs]) for d in dirs]
        yg = [None, None]
        for h0 in range(0, hpg, 2):
            for d in dirs:
                lmats, xmasks = [], []
                for hh in (h0, h0 + 1):
                    hl = head_lane(d, g * hpg + hh)
                    seg = acum[d][:, hl:hl + 1] - acum_t[d][hl:hl + 1, :]
                    lmats.append((cbm[d] * jnp.exp(jnp.where(keep[d], seg, -jnp.inf))).astype(BF16))
                    in_head = jnp.logical_and(glane >= hh * SSD_HEAD_DIM, glane < (hh + 1) * SSD_HEAD_DIM)
                    xmasks.append(jnp.where(in_head, xdt[d][:, gc], 0.0).astype(BF16))
                part = _dot(jnp.concatenate(lmats, axis=1), jnp.concatenate(xmasks, axis=0))
                yg[d] = part if yg[d] is None else yg[d] + part
        for d in dirs:
            hst = h_refs[d][g]
            yoff = _dot(cm[d][:, gs], hst.astype(BF16)) * dstart_x[d][:, gc]
            ys[d].append(yg[d] + yoff)
            h_refs[d][g] = (dstart_x[d][last[d]:last[d] + 1, gc] * hst
                            + _dot_tn(bm[d][:, gs], xend[d][:, gc]))

    @pl.when(t >= ncc)
    def _():
        yf_ref[0] = jnp.concatenate(ys[0], axis=1) + dskip_ref[...] * xs[0]
        yb_ref[0] = jnp.concatenate(ys[1], axis=1)


def _ssd(u_all, small_all, dtb, aneg, dskip, l_x, l_c):
    b = u_all.shape[0]
    q = SSD_CHUNK
    ncx, ncc = l_x // q, l_c // q
    fwd = lambda t: jnp.where(t < ncc, ncx + t, t - ncc)
    bwd = lambda t: ncx + ncc - 1 - t
    chunk_spec = lambda w, order: pl.BlockSpec((1, q, w), lambda bi, t: (bi, order(t), 0))
    y_spec = lambda order: pl.BlockSpec((1, q, SSD_WIDTH), lambda bi, t: (bi, order(jnp.maximum(t, ncc)), 0))
    y_shape = jax.ShapeDtypeStruct((b, l_x, SSD_WIDTH), F32)
    h_buf = pltpu.VMEM((SSD_GROUPS, SSD_STATE, SSD_WIDTH // SSD_GROUPS), F32)
    ex_np = np.zeros((N_DIR, SMALL_W, SSD_WIDTH), np.float32)
    for dd in range(N_DIR):
        for h in range(SSD_HEADS):
            ex_np[dd, DT_LANE0 + SSD_HEADS * dd + h, h * SSD_HEAD_DIM:(h + 1) * SSD_HEAD_DIM] = 1.0
    ex = jnp.asarray(np.concatenate([ex_np, ex_np], axis=1), BF16)
    return pl.pallas_call(
        functools.partial(_ssd_kernel, ncc=ncc),
        out_shape=[y_shape, y_shape],
        grid=(b, ncx + ncc),
        in_specs=[chunk_spec(CONV_CH, fwd), chunk_spec(SMALL_W, fwd),
                  chunk_spec(CONV_CH, bwd), chunk_spec(SMALL_W, bwd),
                  _const_spec(dtb.shape), _const_spec(aneg.shape), _const_spec(dskip.shape),
                  _const_spec(ex.shape)],
        out_specs=[y_spec(fwd), y_spec(bwd)],
        scratch_shapes=[h_buf, h_buf],
        compiler_params=_params(2),
        name="ssd",
    )(u_all, small_all, u_all, small_all, dtb, aneg, dskip, ex)


HEADS_PER_STEP = 2


KEY_BLOCK = 256
ATTN_GROUP = 1


def _attn_kernel(q_ref, k_ref, vt_ref, o_ref):
    n_stored = vt_ref.shape[1]
    kb = vt_ref.shape[3]
    heads = range(HEADS_PER_STEP)
    spans = [(g, min(g + ATTN_GROUP, n_stored)) for g in range(0, n_stored, ATTN_GROUP)]
    nb = len(spans)
    q = [q_ref[0, :, hh * HEAD_PAD:(hh + 1) * HEAD_PAD] for hh in heads]

    def scores(j):
        lo, hi = spans[j]
        return [_dot_nt(k_ref[0, lo * kb:hi * kb, hh * HEAD_PAD:(hh + 1) * HEAD_PAD], q[hh])
                for hh in heads]

    m = [None] * HEADS_PER_STEP
    acc = [None] * HEADS_PER_STEP
    ahead = 2
    pending = [scores(j) for j in range(min(ahead, nb))]
    for j in range(nb):
        s = pending.pop(0)
        if j + ahead < nb:
            pending.append(scores(j + ahead))
        lo, hi = spans[j]
        ones = jnp.ones((2 * SUBLANES, (hi - lo) * kb), BF16)
        for hh in heads:
            vt = jnp.concatenate([vt_ref[0, n, hh * V_DIM:(hh + 1) * V_DIM, :] for n in range(lo, hi)], axis=1)
            vt = jnp.concatenate([vt, ones], axis=0)
            m_blk = jnp.max(s[hh], axis=0, keepdims=True)
            m_new = m_blk if j == 0 else jnp.maximum(m[hh], m_blk)
            part = _dot(vt, jnp.exp2(s[hh] - m_new).astype(BF16))
            acc[hh] = part if j == 0 else acc[hh] * jnp.exp2(m[hh] - m_new) + part
            m[hh] = m_new

    outs = [(acc[hh][0:V_DIM] / acc[hh][V_DIM:V_DIM + 1]).T for hh in heads]
    o_ref[0] = jnp.concatenate(outs, axis=1).astype(BF16)


def _attn(q, k_all, vt_all, tq):
    b, l_x, _ = q.shape
    s_total = k_all.shape[1]
    nb, kb = vt_all.shape[1], vt_all.shape[3]
    hw = HEADS_PER_STEP * HEAD_PAD
    return pl.pallas_call(
        _attn_kernel,
        out_shape=jax.ShapeDtypeStruct((b, l_x, MLA_HEADS * V_DIM), BF16),
        grid=(b, MLA_HEADS // HEADS_PER_STEP, l_x // tq),
        in_specs=[pl.BlockSpec((1, tq, hw), lambda bi, h, i: (bi, i, h)),
                  pl.BlockSpec((1, s_total, hw), lambda bi, h, i: (bi, 0, h)),
                  pl.BlockSpec((1, nb, HEADS_PER_STEP * V_DIM, kb), lambda bi, h, i: (bi, 0, h, 0))],
        out_specs=pl.BlockSpec((1, tq, HEADS_PER_STEP * V_DIM), lambda bi, h, i: (bi, i, h)),
        compiler_params=_params(3),
        name="attn",
    )(q, k_all, vt_all)


def _rope_swap(w):
    parts = []
    for base in (0, QK_ROPE // 2):
        u1 = w[..., base:base + QK_ROPE // 4]
        u2 = w[..., base + QK_ROPE // 4:base + QK_ROPE // 2]
        parts += [-u2, u1]
    return jnp.concatenate(parts, axis=-1)


def _rope_tables(l_x):
    half = QK_ROPE // 2
    inv = (ROPE_BASE ** (-np.arange(0, half, 2, dtype=np.float32) / half)).astype(np.float32)
    rows = l_x // GRID_W
    row = np.repeat(np.arange(rows), GRID_W).astype(np.float32)
    col = (np.arange(rows * GRID_W) % GRID_W).astype(np.float32)
    ar = (row[:, None] * inv).astype(np.float64)
    ac = (col[:, None] * inv).astype(np.float64)
    cos = np.concatenate([np.cos(ar), np.cos(ar), np.cos(ac), np.cos(ac)], axis=1).astype(np.float32)
    sin = np.concatenate([np.sin(ar), np.sin(ar), np.sin(ac), np.sin(ac)], axis=1).astype(np.float32)
    return cos, sin


def _pick_tile(n, pref):
    t = min(pref, n)
    while n % t:
        t //= 2
    return t


def kernel(x, c, ctx, c_ctx, w_ada, b_ada, norm_ffn1, w_ffn1_in, w_ffn1_out, norm_mix, w_in, conv_w, conv_b, dt_bias, a_log, d_skip, ssd_norm, q_norm, w_uq, kv_norm, w_ukv, w_out, norm_ffn2, w_ffn2_in, w_ffn2_out, final_norm):
    b, l_x, d = x.shape
    l_c = ctx.shape[1]
    assert d == D_MODEL and w_ada.shape[0] == 1
    assert l_x % SSD_CHUNK == 0 and l_c % SSD_CHUNK == 0 and l_x % GRID_W == 0
    s_total = l_x + l_c

    c_rows = jnp.concatenate([c, c_ctx[None, :]], axis=0)
    pad = (-c_rows.shape[0]) % SUBLANES
    c_rows = jnp.pad(c_rows, ((0, pad), (0, 0)))
    m_all = _ada(c_rows, w_ada[0], b_ada)[:b + 1].reshape(b + 1, N_MOD, d)

    tm = _pick_tile(l_x, 512)
    tmc = _pick_tile(b * l_c, 512)
    x_tiles = l_x // tm
    m_lat = lambda i: i // x_tiles
    m_ctx = lambda i: b

    wi1, wo1 = w_ffn1_in[0].astype(BF16), w_ffn1_out[0].astype(BF16)
    g1 = norm_ffn1
    x1 = _ffn(x.reshape(b * l_x, d), m_all, m_lat, g1, wi1, wo1, 0, tm)
    c1 = _ffn(ctx.reshape(b * l_c, d), m_all, m_ctx, g1, wi1, wo1, 0, tmc)

    w = w_in[0]
    w_kpe = w[:, SPLIT_CKV:]
    small = jnp.concatenate([w_kpe, _rope_swap(w_kpe), w[:, SPLIT_XBC:SPLIT_DT],
                             jnp.zeros((d, SMALL_W - 2 * QK_ROPE - N_DIR * SSD_HEADS), F32)], axis=1)
    w_all = jnp.concatenate([w[:, :SPLIT_XBC], w[:, SPLIT_DT:SPLIT_CKV], small], axis=1).astype(BF16)
    hq = QK_NOPE + QK_ROPE
    wq = w_uq[0].reshape(Q_LORA, MLA_HEADS, hq)
    zq = jnp.zeros((Q_LORA, MLA_HEADS, HEAD_PAD - hq), F32)
    wqa = jnp.concatenate([wq, zq], axis=2).reshape(Q_LORA, MLA_PAD).astype(BF16)
    wkv = w_ukv[0].reshape(KV_LORA, MLA_HEADS, QK_NOPE + V_DIM)
    wuk = jnp.concatenate([wkv[:, :, :QK_NOPE], jnp.zeros((KV_LORA, MLA_HEADS, HEAD_PAD - QK_NOPE), F32)],
                          axis=2).reshape(KV_LORA, MLA_PAD).astype(BF16)
    wuv = wkv[:, :, QK_NOPE:].reshape(KV_LORA, MLA_HEADS * V_DIM).astype(BF16)
    e_np = np.zeros((SMALL_W, MLA_HEADS, HEAD_PAD), np.float32)
    for i in range(QK_ROPE):
        e_np[i, :, QK_NOPE + i] = 1.0
    e_mat = jnp.asarray(e_np.reshape(SMALL_W, MLA_PAD), BF16)

    cos, sin = _rope_tables(l_x)
    zpad = lambda n: np.zeros((l_x, n), np.float32)
    q_scale = np.float32(MLA_SCALE * math.log2(math.e))
    tqc = jnp.asarray(q_scale * np.concatenate([np.ones((l_x, QK_NOPE), np.float32), cos, zpad(HEAD_PAD - hq)], axis=1))
    sign = np.tile(np.repeat(np.float32([-1.0, 1.0]), QK_ROPE // 4), 2)
    tqs = jnp.asarray(q_scale * np.concatenate([zpad(QK_NOPE), sin * sign, zpad(HEAD_PAD - hq)], axis=1))
    tk_lat = jnp.asarray(np.concatenate([cos, sin, zpad(SMALL_W - 2 * QK_ROPE)], axis=1))
    tk_ctx = jnp.asarray(np.concatenate([np.ones((l_c, QK_ROPE), np.float32),
                                         np.zeros((l_c, SMALL_W - QK_ROPE), np.float32)], axis=1))

    cw = jnp.pad(conv_w[0], ((0, SUBLANES - SSD_CONV), (0, 0)))
    shared = (norm_mix, w_all, cw, conv_b, kv_norm, wuk, wuv, e_mat)
    tmi = _pick_tile(l_x, 1024)
    u_all, small_all, k_all, v_all, z, q = _in_proj(
        x1.reshape(b, l_x, d), m_all, lambda bi: bi, *shared, tk_lat, tmi, 0, s_total,
        q_args=(q_norm, wqa, tqc, tqs))
    tmic = _pick_tile(math.gcd(l_c, l_x), 256)
    u_all, small_all, k_all, v_all = _in_proj(
        c1.reshape(b, l_c, d), m_all, lambda bi: b, *shared, tk_ctx, tmic, l_x, s_total,
        alias_bufs=(u_all, small_all, k_all, v_all))

    lane_pad = lambda v: jnp.pad(v.reshape(1, -1), ((0, 0), (DT_LANE0, SMALL_W - DT_LANE0 - v.size)))
    dtb = lane_pad(dt_bias[0])
    aneg = lane_pad(-jnp.exp(a_log[0].astype(F32)))
    dskip = jnp.repeat(d_skip[0], SSD_HEAD_DIM)[None, :]
    yf, yb = _ssd(u_all, small_all, dtb, aneg, dskip, l_x, l_c)

    mla = _attn(q, k_all, v_all, _pick_tile(l_x, 1024))

    wo = w_out[0]
    ws = wo[:SSD_WIDTH].astype(BF16)
    wm = wo[SSD_WIDTH:].astype(BF16)
    n = b * l_x
    mix = (yf.reshape(n, SSD_WIDTH), yb.reshape(n, SSD_WIDTH), z.reshape(n, SSD_WIDTH),
           mla.reshape(n, MLA_HEADS * V_DIM), ssd_norm, ws, wm, final_norm[None, :])
    out = _ffn(x1, m_all, m_lat, norm_ffn2, w_ffn2_in[0].astype(BF16), w_ffn2_out[0].astype(BF16), 6, tm,
               mix=mix)
    return out.reshape(b, l_x, d)
```

```python
import functools
import math

import jax
import jax.numpy as jnp
import numpy as np
from jax import lax
from jax.experimental import pallas as pl
from jax.experimental.pallas import tpu as pltpu

F32 = jnp.float32
BF16 = jnp.bfloat16

D_MODEL = 1024
GRID_W = 64
N_MOD = 9
D_FF = 2816
EPS = 1e-6
SSD_HEADS = 8
SSD_HEAD_DIM = 64
SSD_WIDTH = SSD_HEADS * SSD_HEAD_DIM
SSD_GROUPS = 2
SSD_STATE = 128
SSD_CONV = 5
SSD_CHUNK = 128
SSD_BLOCK_CHUNKS = 2
N_DIR = 2
CONV_CH = SSD_WIDTH + 2 * SSD_GROUPS * SSD_STATE
MLA_HEADS = 8
QK_NOPE = 64
QK_ROPE = 32
V_DIM = 64
Q_LORA = 768
KV_LORA = 256
MLA_SCALE = (QK_NOPE + QK_ROPE) ** -0.5
ROPE_BASE = 10000.0
SPLIT_Z = SSD_WIDTH
SPLIT_XBC = SPLIT_Z + CONV_CH
SPLIT_DT = SPLIT_XBC + N_DIR * SSD_HEADS
SPLIT_CQ = SPLIT_DT + Q_LORA
SPLIT_CKV = SPLIT_CQ + KV_LORA

LANES = 128
SUBLANES = 8
HEAD_PAD = 128
MLA_PAD = MLA_HEADS * HEAD_PAD
FF_CHUNK = 256
SMALL_W = LANES
DT_LANE0 = 0
KPE_LANE0 = QK_NOPE
W_ALL = SSD_WIDTH + CONV_CH + Q_LORA + KV_LORA + SMALL_W
VMEM_LIMIT = 56 * 1024 * 1024


def _dot(a, b):
    return jnp.dot(a, b, preferred_element_type=F32)


def _dot_nt(a, b):
    return lax.dot_general(a, b, (((1,), (1,)), ((), ())), preferred_element_type=F32)


def _dot_tn(a, b):
    return lax.dot_general(a, b, (((0,), (0,)), ((), ())), preferred_element_type=F32)


def _silu(x):
    return x * (1.0 / (1.0 + jnp.exp(-x)))


def _rms(x, g):
    return x * lax.rsqrt(jnp.mean(x * x, axis=-1, keepdims=True) + EPS) * g


def _const_spec(shape):
    zeros = (0,) * len(shape)
    return pl.BlockSpec(shape, lambda *_: zeros, pipeline_mode=pl.Buffered(1))


def _params(n_axes, flags=None):
    return pltpu.CompilerParams(dimension_semantics=("arbitrary",) * n_axes,
                                vmem_limit_bytes=VMEM_LIMIT, flags=flags)


def _ada_kernel(c_ref, w_ref, b_ref, o_ref):
    s = _silu(c_ref[...]).astype(BF16)
    o_ref[...] = _dot(s, w_ref[...].astype(BF16)) + b_ref[...]


def _ada(c_rows, w, b):
    rows, d = c_rows.shape
    n = w.shape[1]
    tn = 1024
    return pl.pallas_call(
        _ada_kernel,
        out_shape=jax.ShapeDtypeStruct((rows, n), F32),
        grid=(n // tn,),
        in_specs=[pl.BlockSpec((rows, d), lambda j: (0, 0)),
                  pl.BlockSpec((d, tn), lambda j: (0, j)),
                  pl.BlockSpec((1, tn), lambda j: (0, j))],
        out_specs=pl.BlockSpec((rows, tn), lambda j: (0, j)),
        compiler_params=_params(1),
        name="ada",
    )(c_rows, w, b)


def _ffn_kernel(x_ref, m_ref, g_ref, wi_ref, wo_ref, *rest, k, final):
    x = x_ref[...]
    m = m_ref[0]
    if final:
        yf_ref, yb_ref, z_ref, a_ref, sn_ref, ws_ref, wm_ref, fg_ref, o_ref, acc_ref = rest
        y = (yf_ref[...] + yb_ref[...]) * _silu(z_ref[...])
        ssd = _rms(y, sn_ref[...]).astype(BF16)
        x = x + m[k - 1:k] * (_dot(ssd, ws_ref[...]) + _dot(a_ref[...], wm_ref[...]))
    else:
        o_ref, acc_ref = rest
    shift, scale, gate = m[k:k + 1], m[k + 1:k + 2], m[k + 2:k + 3]
    hn = (_rms(x, g_ref[...]) * (1 + scale) + shift).astype(BF16)
    for j in range(D_FF // FF_CHUNK):
        lo = j * FF_CHUNK
        g = _dot(hn, wi_ref[:, lo:lo + FF_CHUNK])
        u = _dot(hn, wi_ref[:, D_FF + lo:D_FF + lo + FF_CHUNK])
        a = (_silu(g) * u).astype(BF16)
        part = _dot(a, wo_ref[lo:lo + FF_CHUNK, :])
        if j == 0:
            acc_ref[...] = part
        else:
            acc_ref[...] += part
    y = x + 0.5 * gate * acc_ref[...]
    if final:
        y = _rms(y, fg_ref[...])
    o_ref[...] = y


def _ffn(x2d, m_all, m_index, g, wi, wo, k, tm, mix=None):
    n, d = x2d.shape
    final = mix is not None
    tok = lambda c: pl.BlockSpec((tm, c), lambda i: (i, 0))
    in_specs = [tok(d),
                pl.BlockSpec((1, N_MOD, d), lambda i: (m_index(i), 0, 0)),
                _const_spec((1, d)),
                _const_spec(wi.shape),
                _const_spec(wo.shape)]
    args = [x2d, m_all, g, wi, wo]
    if final:
        in_specs += [tok(a.shape[1]) for a in mix[:4]] + [_const_spec(a.shape) for a in mix[4:]]
        args += list(mix)
    return pl.pallas_call(
        functools.partial(_ffn_kernel, k=k, final=final),
        out_shape=jax.ShapeDtypeStruct((n, d), F32),
        grid=(n // tm,),
        in_specs=in_specs,
        out_specs=pl.BlockSpec((tm, d), lambda i: (i, 0)),
        scratch_shapes=[pltpu.VMEM((tm, d), F32)],
        compiler_params=_params(1),
        name="ffn_final" if final else "ffn",
    )(*args)


def _in_proj_kernel(x_ref, xp_ref, xn_ref, m_ref, g_ref, w_ref, cw_ref, cb_ref, kvn_ref, wuk_ref,
                    wuv_ref, tk_ref, *rest, with_q, n_alias):
    if with_q:
        qn_ref, wqa_ref, tqc_ref, tqs_ref = rest[:4]
        rest = rest[4:]
    rest = rest[n_alias:]
    u_ref, small_ref, k_ref, v_ref = rest[:4]
    if with_q:
        z_ref, q_ref = rest[4:6]
    ext_ref = rest[-1]
    m = m_ref[0]
    tm = x_ref.shape[1]
    i = pl.program_id(1)
    x_ext = jnp.concatenate([x_ref[0], xp_ref[0], xn_ref[0]], axis=0)
    hn_ext = (_rms(x_ext, g_ref[...]) * (1 + m[4:5]) + m[3:4]).astype(BF16)
    hn = hn_ext[0:tm]
    o_xbc = SSD_WIDTH
    o_cq = o_xbc + CONV_CH
    o_ckv = o_cq + Q_LORA
    o_small = o_ckv + KV_LORA
    xbc = _dot(hn_ext, w_ref[:, o_xbc:o_cq])
    has_prev = (i > 0).astype(F32)
    has_next = (i < pl.num_programs(1) - 1).astype(F32)
    ext_ref[0:SUBLANES] = xbc[tm:tm + SUBLANES] * has_prev
    ext_ref[SUBLANES:SUBLANES + tm] = xbc[0:tm]
    ext_ref[SUBLANES + tm:] = xbc[tm + SUBLANES:] * has_next
    half = SSD_CONV // 2
    n_conv_chunks = 4
    cc = CONV_CH // n_conv_chunks

    def conv_chunk(j):
        cols = slice(j * cc, (j + 1) * cc)
        conv = cb_ref[:, cols]
        for tap in range(SSD_CONV):
            conv = conv + ext_ref[pl.ds(SUBLANES - half + tap, tm), cols] * cw_ref[tap:tap + 1, cols]
        u_ref[0, :, cols] = _silu(conv).astype(BF16)

    small = _dot(hn, w_ref[:, o_small:o_small + SMALL_W])
    small_ref[0] = small
    ckv = _rms(_dot(hn, w_ref[:, o_ckv:o_small]), kvn_ref[...]).astype(BF16)
    conv_chunk(0)
    t = small * tk_ref[...]
    lane = lax.broadcasted_iota(jnp.int32, (tm, LANES), 1)
    on_rope = jnp.logical_and(lane >= KPE_LANE0, lane < KPE_LANE0 + QK_ROPE)
    kpe = jnp.where(on_rope, t + pltpu.roll(t, QK_ROPE, axis=1), 0.0)
    k_nope = _dot(ckv, wuk_ref[...])
    for h in range(MLA_HEADS):
        hs = slice(h * HEAD_PAD, (h + 1) * HEAD_PAD)
        k_ref[0, :, hs] = (k_nope[:, hs] + kpe).astype(BF16)
    vt = _dot(ckv, wuv_ref[...]).T.astype(BF16)
    for n in range(tm // KEY_BLOCK):
        v_ref[0, n] = vt[:, n * KEY_BLOCK:(n + 1) * KEY_BLOCK]
    conv_chunk(1)
    if with_q:
        z_ref[0] = _dot(hn, w_ref[:, 0:o_xbc])
        cq = _rms(_dot(hn, w_ref[:, o_cq:o_ckv]), qn_ref[...]).astype(BF16)
        conv_chunk(2)
        qa = _dot(cq, wqa_ref[...])
        conv_chunk(3)
        tqc, tqs = tqc_ref[...], tqs_ref[...]
        first_half = lax.broadcasted_iota(jnp.int32, (tm, HEAD_PAD), 1) % (QK_ROPE // 2) < QK_ROPE // 4
        for h in range(MLA_HEADS):
            hs = slice(h * HEAD_PAD, (h + 1) * HEAD_PAD)
            qh = qa[:, hs]
            partner = jnp.where(first_half, pltpu.roll(qh, HEAD_PAD - QK_ROPE // 4, axis=1),
                                pltpu.roll(qh, QK_ROPE // 4, axis=1))
            q_ref[0, :, hs] = (qh * tqc + partner * tqs).astype(BF16)
    else:
        conv_chunk(2)
        conv_chunk(3)


def _in_proj(h3d, m_all, m_index, g, w_all, cw, cb, kvn, wuk, wuv, tk, tm, row0, s_total,
             q_args=None, alias_bufs=()):
    b, rows, d = h3d.shape
    with_q = q_args is not None
    blk0 = row0 // tm
    hb, n8 = tm // SUBLANES, rows // SUBLANES
    tok = lambda c: pl.BlockSpec((1, tm, c), lambda bi, i: (bi, i, 0))
    comb = lambda c: pl.BlockSpec((1, tm, c), lambda bi, i: (bi, blk0 + i, 0))
    in_specs = [tok(d),
                pl.BlockSpec((1, SUBLANES, d), lambda bi, i: (bi, jnp.maximum(i * hb - 1, 0), 0)),
                pl.BlockSpec((1, SUBLANES, d), lambda bi, i: (bi, jnp.minimum(i * hb + hb, n8 - 1), 0)),
                pl.BlockSpec((1, N_MOD, d), lambda bi, i: (m_index(bi), 0, 0)),
                _const_spec(g.shape), _const_spec(w_all.shape), _const_spec(cw.shape), _const_spec(cb.shape),
                _const_spec(kvn.shape), _const_spec(wuk.shape), _const_spec(wuv.shape),
                pl.BlockSpec((tm, LANES), lambda bi, i: (i, 0))]
    args = [h3d, h3d, h3d, m_all, g, w_all, cw, cb, kvn, wuk, wuv, tk]
    out_shape = [jax.ShapeDtypeStruct((b, s_total, CONV_CH), BF16),
                 jax.ShapeDtypeStruct((b, s_total, SMALL_W), F32),
                 jax.ShapeDtypeStruct((b, s_total, MLA_PAD), BF16),
                 jax.ShapeDtypeStruct((b, s_total // KEY_BLOCK, MLA_HEADS * V_DIM, KEY_BLOCK), BF16)]
    out_specs = [comb(CONV_CH), comb(SMALL_W), comb(MLA_PAD),
                 pl.BlockSpec((1, tm // KEY_BLOCK, MLA_HEADS * V_DIM, KEY_BLOCK),
                              lambda bi, i: (bi, blk0 + i, 0, 0))]
    if with_q:
        qn, wqa, tqc, tqs = q_args
        in_specs += [_const_spec(qn.shape), _const_spec(wqa.shape),
                     pl.BlockSpec((tm, LANES), lambda bi, i: (i, 0)),
                     pl.BlockSpec((tm, LANES), lambda bi, i: (i, 0))]
        args += [qn, wqa, tqc, tqs]
        out_shape += [jax.ShapeDtypeStruct((b, rows, SSD_WIDTH), F32),
                      jax.ShapeDtypeStruct((b, rows, MLA_PAD), BF16)]
        out_specs += [tok(SSD_WIDTH), tok(MLA_PAD)]
    aliases = {len(args) + j: j for j in range(len(alias_bufs))}
    in_specs += [pl.BlockSpec(memory_space=pl.ANY)] * len(alias_bufs)
    args += list(alias_bufs)
    return pl.pallas_call(
        functools.partial(_in_proj_kernel, with_q=with_q, n_alias=len(alias_bufs)),
        out_shape=out_shape,
        grid=(b, rows // tm),
        in_specs=in_specs,
        out_specs=out_specs,
        input_output_aliases=aliases,
        scratch_shapes=[pltpu.VMEM((tm + 2 * SUBLANES, CONV_CH), F32)],
        compiler_params=_params(2),
        name="in_proj_q" if with_q else "in_proj_ctx",
    )(*args)


def _ssd_kernel(uf_ref, sf_ref, ub_ref, sb_ref, dtb_ref, aneg_ref, dskip_ref, ex_ref, yf_ref, yb_ref,
                hf_ref, hb_ref, *, ncb):
    q = SSD_CHUNK
    t = pl.program_id(1)
    n_sub = uf_ref.shape[1] // q
    u_refs, small_refs, h_refs, y_refs = (uf_ref, ub_ref), (sf_ref, sb_ref), (hf_ref, hb_ref), (yf_ref, yb_ref)
    lanes = [(d, k if d == 0 else n_sub - 1 - k) for k in range(n_sub) for d in (0, 1)]
    rows = [slice(sub * q, (sub + 1) * q) for _, sub in lanes]
    ids = range(len(lanes))

    @pl.when(t == 0)
    def _():
        hf_ref[...] = jnp.zeros_like(hf_ref)
        hb_ref[...] = jnp.zeros_like(hb_ref)

    o_b = SSD_WIDTH
    o_c = SSD_WIDTH + SSD_GROUPS * SSD_STATE
    xs = [u_refs[d][0, rows[i], 0:o_b].astype(F32) for i, (d, _) in enumerate(lanes)]
    bm = [u_refs[d][0, rows[i], o_b:o_c] for i, (d, _) in enumerate(lanes)]
    cm = [u_refs[d][0, rows[i], o_c:CONV_CH] for i, (d, _) in enumerate(lanes)]

    ri = lax.broadcasted_iota(jnp.int32, (q, q), 0)
    ci = lax.broadcasted_iota(jnp.int32, (q, q), 1)
    keep = [ri >= ci, ri <= ci]
    last = [q - 1, 0]
    tri = [jnp.concatenate([jnp.where(keep[d], 1.0, 0.0).astype(BF16)] * 3, axis=1) for d in (0, 1)]
    dtv, parts = [], []
    for i, (d, _) in enumerate(lanes):
        pre = small_refs[d][0, rows[i], :] + dtb_ref[...]
        dtv.append(jnp.maximum(pre, 0.0) + jnp.log(1.0 + jnp.exp(-jnp.abs(pre))))
        a = dtv[i] * aneg_ref[...]
        a1 = a.astype(BF16)
        r1 = a - a1.astype(F32)
        a2 = r1.astype(BF16)
        a3 = (r1 - a2.astype(F32)).astype(BF16)
        parts.append(jnp.concatenate([a1, a2, a3], axis=0))
    acum = [_dot(tri[lanes[i][0]], parts[i]) for i in ids]
    acum_t = [acum[i].T for i in ids]
    dend = [jnp.exp(acum[i][last[d]:last[d] + 1, :] - acum[i]) for i, (d, _) in enumerate(lanes)]
    dstart = [jnp.exp(acum[i]) for i in ids]

    head_lane = lambda d, h: DT_LANE0 + SSD_HEADS * d + h

    def expand(arr, d):
        hi = arr.astype(BF16)
        lo = (arr - hi.astype(F32)).astype(BF16)
        return _dot(jnp.concatenate([hi, lo], axis=1), ex_ref[d])

    dstart_x = [expand(dstart[i], d) for i, (d, _) in enumerate(lanes)]
    xdt = [xs[i] * expand(dtv[i], d) for i, (d, _) in enumerate(lanes)]
    xend = [(xdt[i] * expand(dend[i], d)).astype(BF16) for i, (d, _) in enumerate(lanes)]
    gw = SSD_WIDTH // SSD_GROUPS
    hpg = SSD_HEADS // SSD_GROUPS
    glane = lax.broadcasted_iota(jnp.int32, (q, gw), 1)
    ys = [[] for _ in ids]
    for g in range(SSD_GROUPS):
        gs = slice(g * SSD_STATE, (g + 1) * SSD_STATE)
        gc = slice(g * gw, (g + 1) * gw)
        cbm = [_dot_nt(cm[i][:, gs], bm[i][:, gs]) for i in ids]
        yg = [None for _ in ids]
        for h0 in range(0, hpg, 2):
            for i, (d, _) in enumerate(lanes):
                lmats, xmasks = [], []
                for hh in (h0, h0 + 1):
                    hl = head_lane(d, g * hpg + hh)
                    seg = acum[i][:, hl:hl + 1] - acum_t[i][hl:hl + 1, :]
                    lmats.append((cbm[i] * jnp.exp(jnp.where(keep[d], seg, -jnp.inf))).astype(BF16))
                    in_head = jnp.logical_and(glane >= hh * SSD_HEAD_DIM, glane < (hh + 1) * SSD_HEAD_DIM)
                    xmasks.append(jnp.where(in_head, xdt[i][:, gc], 0.0).astype(BF16))
                part = _dot(jnp.concatenate(lmats, axis=1), jnp.concatenate(xmasks, axis=0))
                yg[i] = part if yg[i] is None else yg[i] + part
        for i, (d, _) in enumerate(lanes):
            hst = h_refs[d][g]
            yoff = _dot(cm[i][:, gs], hst.astype(BF16)) * dstart_x[i][:, gc]
            ys[i].append(yg[i] + yoff)
            h_refs[d][g] = (dstart_x[i][last[d]:last[d] + 1, gc] * hst
                            + _dot_tn(bm[i][:, gs], xend[i][:, gc]))

    @pl.when(t >= ncb)
    def _():
        for i, (d, _) in enumerate(lanes):
            y = jnp.concatenate(ys[i], axis=1)
            if d == 0:
                y = y + dskip_ref[...] * xs[i]
            y_refs[d][0, rows[i], :] = y


def _ssd(u_all, small_all, dtb, aneg, dskip, l_x, l_c):
    b = u_all.shape[0]
    n_sub = SSD_BLOCK_CHUNKS if (l_x % (SSD_BLOCK_CHUNKS * SSD_CHUNK) == 0
                                 and l_c % (SSD_BLOCK_CHUNKS * SSD_CHUNK) == 0) else 1
    q = n_sub * SSD_CHUNK
    ncx, ncc = l_x // q, l_c // q
    fwd = lambda t: jnp.where(t < ncc, ncx + t, t - ncc)
    bwd = lambda t: ncx + ncc - 1 - t
    chunk_spec = lambda w, order: pl.BlockSpec((1, q, w), lambda bi, t: (bi, order(t), 0))
    y_spec = lambda order: pl.BlockSpec((1, q, SSD_WIDTH), lambda bi, t: (bi, order(jnp.maximum(t, ncc)), 0))
    y_shape = jax.ShapeDtypeStruct((b, l_x, SSD_WIDTH), F32)
    h_buf = pltpu.VMEM((SSD_GROUPS, SSD_STATE, SSD_WIDTH // SSD_GROUPS), F32)
    ex_np = np.zeros((N_DIR, SMALL_W, SSD_WIDTH), np.float32)
    for dd in range(N_DIR):
        for h in range(SSD_HEADS):
            ex_np[dd, DT_LANE0 + SSD_HEADS * dd + h, h * SSD_HEAD_DIM:(h + 1) * SSD_HEAD_DIM] = 1.0
    ex = jnp.asarray(np.concatenate([ex_np, ex_np], axis=1), BF16)
    return pl.pallas_call(
        functools.partial(_ssd_kernel, ncb=ncc),
        out_shape=[y_shape, y_shape],
        grid=(b, ncx + ncc),
        in_specs=[chunk_spec(CONV_CH, fwd), chunk_spec(SMALL_W, fwd),
                  chunk_spec(CONV_CH, bwd), chunk_spec(SMALL_W, bwd),
                  _const_spec(dtb.shape), _const_spec(aneg.shape), _const_spec(dskip.shape),
                  _const_spec(ex.shape)],
        out_specs=[y_spec(fwd), y_spec(bwd)],
        scratch_shapes=[h_buf, h_buf],
        compiler_params=_params(2),
        name="ssd",
    )(u_all, small_all, u_all, small_all, dtb, aneg, dskip, ex)


HEADS_PER_STEP = 2


KEY_BLOCK = 256
ATTN_GROUP = 1


def _attn_kernel(q_ref, k_ref, vt_ref, o_ref):
    n_stored = vt_ref.shape[1]
    kb = vt_ref.shape[3]
    heads = range(HEADS_PER_STEP)
    spans = [(g, min(g + ATTN_GROUP, n_stored)) for g in range(0, n_stored, ATTN_GROUP)]
    nb = len(spans)
    q = [q_ref[0, :, hh * HEAD_PAD:(hh + 1) * HEAD_PAD] for hh in heads]

    def scores(j):
        lo, hi = spans[j]
        return [_dot_nt(k_ref[0, lo * kb:hi * kb, hh * HEAD_PAD:(hh + 1) * HEAD_PAD], q[hh])
                for hh in heads]

    m = [None] * HEADS_PER_STEP
    acc = [None] * HEADS_PER_STEP
    ahead = 2
    pending = [scores(j) for j in range(min(ahead, nb))]
    for j in range(nb):
        s = pending.pop(0)
        if j + ahead < nb:
            pending.append(scores(j + ahead))
        lo, hi = spans[j]
        ones = jnp.ones((2 * SUBLANES, (hi - lo) * kb), BF16)
        for hh in heads:
            vt = jnp.concatenate([vt_ref[0, n, hh * V_DIM:(hh + 1) * V_DIM, :] for n in range(lo, hi)], axis=1)
            vt = jnp.concatenate([vt, ones], axis=0)
            m_blk = jnp.max(s[hh], axis=0, keepdims=True)
            m_new = m_blk if j == 0 else jnp.maximum(m[hh], m_blk)
            part = _dot(vt, jnp.exp2(s[hh] - m_new).astype(BF16))
            acc[hh] = part if j == 0 else acc[hh] * jnp.exp2(m[hh] - m_new) + part
            m[hh] = m_new

    outs = [(acc[hh][0:V_DIM] / acc[hh][V_DIM:V_DIM + 1]).T for hh in heads]
    o_ref[0] = jnp.concatenate(outs, axis=1).astype(BF16)


def _attn(q, k_all, vt_all, tq):
    b, l_x, _ = q.shape
    s_total = k_all.shape[1]
    nb, kb = vt_all.shape[1], vt_all.shape[3]
    hw = HEADS_PER_STEP * HEAD_PAD
    return pl.pallas_call(
        _attn_kernel,
        out_shape=jax.ShapeDtypeStruct((b, l_x, MLA_HEADS * V_DIM), BF16),
        grid=(b, MLA_HEADS // HEADS_PER_STEP, l_x // tq),
        in_specs=[pl.BlockSpec((1, tq, hw), lambda bi, h, i: (bi, i, h)),
                  pl.BlockSpec((1, s_total, hw), lambda bi, h, i: (bi, 0, h)),
                  pl.BlockSpec((1, nb, HEADS_PER_STEP * V_DIM, kb), lambda bi, h, i: (bi, 0, h, 0))],
        out_specs=pl.BlockSpec((1, tq, HEADS_PER_STEP * V_DIM), lambda bi, h, i: (bi, i, h)),
        compiler_params=_params(3),
        name="attn",
    )(q, k_all, vt_all)


def _rope_swap(w):
    parts = []
    for base in (0, QK_ROPE // 2):
        u1 = w[..., base:base + QK_ROPE // 4]
        u2 = w[..., base + QK_ROPE // 4:base + QK_ROPE // 2]
        parts += [-u2, u1]
    return jnp.concatenate(parts, axis=-1)


def _rope_tables(l_x):
    half = QK_ROPE // 2
    inv = (ROPE_BASE ** (-np.arange(0, half, 2, dtype=np.float32) / half)).astype(np.float32)
    rows = l_x // GRID_W
    row = np.repeat(np.arange(rows), GRID_W).astype(np.float32)
    col = (np.arange(rows * GRID_W) % GRID_W).astype(np.float32)
    ar = (row[:, None] * inv).astype(np.float64)
    ac = (col[:, None] * inv).astype(np.float64)
    cos = np.concatenate([np.cos(ar), np.cos(ar), np.cos(ac), np.cos(ac)], axis=1).astype(np.float32)
    sin = np.concatenate([np.sin(ar), np.sin(ar), np.sin(ac), np.sin(ac)], axis=1).astype(np.float32)
    return cos, sin


def _pick_tile(n, pref):
    t = min(pref, n)
    while n % t:
        t //= 2
    return t


def kernel(x, c, ctx, c_ctx, w_ada, b_ada, norm_ffn1, w_ffn1_in, w_ffn1_out, norm_mix, w_in, conv_w, conv_b, dt_bias, a_log, d_skip, ssd_norm, q_norm, w_uq, kv_norm, w_ukv, w_out, norm_ffn2, w_ffn2_in, w_ffn2_out, final_norm):
    b, l_x, d = x.shape
    l_c = ctx.shape[1]
    assert d == D_MODEL and w_ada.shape[0] == 1
    assert l_x % SSD_CHUNK == 0 and l_c % SSD_CHUNK == 0 and l_x % GRID_W == 0
    s_total = l_x + l_c

    c_rows = jnp.concatenate([c, c_ctx[None, :]], axis=0)
    pad = (-c_rows.shape[0]) % SUBLANES
    c_rows = jnp.pad(c_rows, ((0, pad), (0, 0)))
    m_all = _ada(c_rows, w_ada[0], b_ada)[:b + 1].reshape(b + 1, N_MOD, d)

    tm = _pick_tile(l_x, 512)
    tmc = _pick_tile(b * l_c, 512)
    x_tiles = l_x // tm
    m_lat = lambda i: i // x_tiles
    m_ctx = lambda i: b

    wi1, wo1 = w_ffn1_in[0].astype(BF16), w_ffn1_out[0].astype(BF16)
    g1 = norm_ffn1
    x1 = _ffn(x.reshape(b * l_x, d), m_all, m_lat, g1, wi1, wo1, 0, tm)
    c1 = _ffn(ctx.reshape(b * l_c, d), m_all, m_ctx, g1, wi1, wo1, 0, tmc)

    w = w_in[0]
    w_kpe = w[:, SPLIT_CKV:]
    n_dt = N_DIR * SSD_HEADS
    small = jnp.concatenate([w[:, SPLIT_XBC:SPLIT_DT], jnp.zeros((d, KPE_LANE0 - QK_ROPE - n_dt), F32),
                             _rope_swap(w_kpe), w_kpe,
                             jnp.zeros((d, SMALL_W - KPE_LANE0 - QK_ROPE), F32)], axis=1)
    w_all = jnp.concatenate([w[:, :SPLIT_XBC], w[:, SPLIT_DT:SPLIT_CKV], small], axis=1).astype(BF16)
    hq = QK_NOPE + QK_ROPE
    wq = w_uq[0].reshape(Q_LORA, MLA_HEADS, hq)
    zq = jnp.zeros((Q_LORA, MLA_HEADS, HEAD_PAD - hq), F32)
    wqa = jnp.concatenate([wq, zq], axis=2).reshape(Q_LORA, MLA_PAD).astype(BF16)
    wkv = w_ukv[0].reshape(KV_LORA, MLA_HEADS, QK_NOPE + V_DIM)
    wuk = jnp.concatenate([wkv[:, :, :QK_NOPE], jnp.zeros((KV_LORA, MLA_HEADS, HEAD_PAD - QK_NOPE), F32)],
                          axis=2).reshape(KV_LORA, MLA_PAD).astype(BF16)
    wuv = wkv[:, :, QK_NOPE:].reshape(KV_LORA, MLA_HEADS * V_DIM).astype(BF16)
    cos, sin = _rope_tables(l_x)
    zpad = lambda n: np.zeros((l_x, n), np.float32)
    q_scale = np.float32(MLA_SCALE * math.log2(math.e))
    tqc = jnp.asarray(q_scale * np.concatenate([np.ones((l_x, QK_NOPE), np.float32), cos, zpad(HEAD_PAD - hq)], axis=1))
    sign = np.tile(np.repeat(np.float32([-1.0, 1.0]), QK_ROPE // 4), 2)
    tqs = jnp.asarray(q_scale * np.concatenate([zpad(QK_NOPE), sin * sign, zpad(HEAD_PAD - hq)], axis=1))
    tk_lat = jnp.asarray(np.concatenate([zpad(KPE_LANE0 - QK_ROPE), sin, cos,
                                         zpad(SMALL_W - KPE_LANE0 - QK_ROPE)], axis=1))
    tk_ctx_np = np.zeros((l_c, SMALL_W), np.float32)
    tk_ctx_np[:, KPE_LANE0:KPE_LANE0 + QK_ROPE] = 1.0
    tk_ctx = jnp.asarray(tk_ctx_np)

    cw = jnp.pad(conv_w[0], ((0, SUBLANES - SSD_CONV), (0, 0)))
    shared = (norm_mix, w_all, cw, conv_b, kv_norm, wuk, wuv)
    tmi = _pick_tile(l_x, 1024)
    u_all, small_all, k_all, v_all, z, q = _in_proj(
        x1.reshape(b, l_x, d), m_all, lambda bi: bi, *shared, tk_lat, tmi, 0, s_total,
        q_args=(q_norm, wqa, tqc, tqs))
    tmic = _pick_tile(math.gcd(l_c, l_x), 256)
    u_all, small_all, k_all, v_all = _in_proj(
        c1.reshape(b, l_c, d), m_all, lambda bi: b, *shared, tk_ctx, tmic, l_x, s_total,
        alias_bufs=(u_all, small_all, k_all, v_all))

    lane_pad = lambda v: jnp.pad(v.reshape(1, -1), ((0, 0), (DT_LANE0, SMALL_W - DT_LANE0 - v.size)))
    dtb = lane_pad(dt_bias[0])
    aneg = lane_pad(-jnp.exp(a_log[0].astype(F32)))
    dskip = jnp.repeat(d_skip[0], SSD_HEAD_DIM)[None, :]
    yf, yb = _ssd(u_all, small_all, dtb, aneg, dskip, l_x, l_c)

    mla = _attn(q, k_all, v_all, _pick_tile(l_x, 1024))

    wo = w_out[0]
    ws = wo[:SSD_WIDTH].astype(BF16)
    wm = wo[SSD_WIDTH:].astype(BF16)
    n = b * l_x
    mix = (yf.reshape(n, SSD_WIDTH), yb.reshape(n, SSD_WIDTH), z.reshape(n, SSD_WIDTH),
           mla.reshape(n, MLA_HEADS * V_DIM), ssd_norm, ws, wm, final_norm[None, :])
    out = _ffn(x1, m_all, m_lat, norm_ffn2, w_ffn2_in[0].astype(BF16), w_ffn2_out[0].astype(BF16), 6, tm,
               mix=mix)
    return out.reshape(b, l_x, d)
```

```python
import functools
import math

import jax
import jax.numpy as jnp
import numpy as np
from jax import lax
from jax.experimental import pallas as pl
from jax.experimental.pallas import tpu as pltpu

F32 = jnp.float32
BF16 = jnp.bfloat16

D_MODEL = 1024
GRID_W = 64
N_MOD = 9
D_FF = 2816
EPS = 1e-6
SSD_HEADS = 8
SSD_HEAD_DIM = 64
SSD_WIDTH = SSD_HEADS * SSD_HEAD_DIM
SSD_GROUPS = 2
SSD_STATE = 128
SSD_CONV = 5
SSD_CHUNK = 128
N_DIR = 2
CONV_CH = SSD_WIDTH + 2 * SSD_GROUPS * SSD_STATE
MLA_HEADS = 8
QK_NOPE = 64
QK_ROPE = 32
V_DIM = 64
Q_LORA = 768
KV_LORA = 256
MLA_SCALE = (QK_NOPE + QK_ROPE) ** -0.5
ROPE_BASE = 10000.0
SPLIT_XBC = SSD_WIDTH + CONV_CH
SPLIT_DT = SPLIT_XBC + N_DIR * SSD_HEADS
SPLIT_CQ = SPLIT_DT + Q_LORA
SPLIT_CKV = SPLIT_CQ + KV_LORA

LANES = 128
SUBLANES = 8
VMEM_LIMIT = 56 * 1024 * 1024

HEAD_PAD = LANES
MLA_PAD = MLA_HEADS * HEAD_PAD
SMALL_W = LANES
DT_LANE0 = 0
KPE_LANE0 = QK_NOPE
KEY_BLOCK = 256

ADA_COLS = 1024
FFN_TILE = 512
FF_CHUNK = 256
PROJ_TILE = 1024
PROJ_CTX_TILE = 256
SSD_BLOCK_CHUNKS = 2
ATTN_Q_TILE = 1024
ATTN_HEADS = LANES // V_DIM
ATTN_LOOKAHEAD = 2


def _dot(a, b):
    return jnp.dot(a, b, preferred_element_type=F32)


def _dot_nt(a, b):
    return lax.dot_general(a, b, (((1,), (1,)), ((), ())), preferred_element_type=F32)


def _dot_tn(a, b):
    return lax.dot_general(a, b, (((0,), (0,)), ((), ())), preferred_element_type=F32)


def _silu(x):
    return x * (1.0 / (1.0 + jnp.exp(-x)))


def _rms(x, g):
    return x * lax.rsqrt(jnp.mean(x * x, axis=-1, keepdims=True) + EPS) * g


def _const_spec(shape):
    zeros = (0,) * len(shape)
    return pl.BlockSpec(shape, lambda *_: zeros, pipeline_mode=pl.Buffered(1))


def _params(n_axes):
    return pltpu.CompilerParams(dimension_semantics=("arbitrary",) * n_axes,
                                vmem_limit_bytes=VMEM_LIMIT)


def _pick_tile(n, pref):
    t = min(pref, n)
    while n % t:
        t //= 2
    return t


def _ada_kernel(c_ref, w_ref, b_ref, o_ref):
    s = _silu(c_ref[...]).astype(BF16)
    o_ref[...] = _dot(s, w_ref[...].astype(BF16)) + b_ref[...]


def _ada(c_rows, w, b):
    rows, d = c_rows.shape
    n = w.shape[1]
    tn = _pick_tile(n, ADA_COLS)
    return pl.pallas_call(
        _ada_kernel,
        out_shape=jax.ShapeDtypeStruct((rows, n), F32),
        grid=(n // tn,),
        in_specs=[pl.BlockSpec((rows, d), lambda j: (0, 0)),
                  pl.BlockSpec((d, tn), lambda j: (0, j)),
                  pl.BlockSpec((1, tn), lambda j: (0, j))],
        out_specs=pl.BlockSpec((rows, tn), lambda j: (0, j)),
        compiler_params=_params(1),
        name="ada",
    )(c_rows, w, b)


def _ffn_kernel(x_ref, m_ref, g_ref, wi_ref, wo_ref, *rest, k, final):
    x = x_ref[...]
    m = m_ref[0]
    if final:
        yf_ref, yb_ref, z_ref, a_ref, sn_ref, ws_ref, wm_ref, fg_ref, o_ref, acc_ref = rest
        y = (yf_ref[...] + yb_ref[...]) * _silu(z_ref[...])
        ssd = _rms(y, sn_ref[...]).astype(BF16)
        x = x + m[k - 1:k] * (_dot(ssd, ws_ref[...]) + _dot(a_ref[...], wm_ref[...]))
    else:
        o_ref, acc_ref = rest
    shift, scale, gate = m[k:k + 1], m[k + 1:k + 2], m[k + 2:k + 3]
    hn = (_rms(x, g_ref[...]) * (1 + scale) + shift).astype(BF16)
    for j in range(D_FF // FF_CHUNK):
        lo = j * FF_CHUNK
        g = _dot(hn, wi_ref[:, lo:lo + FF_CHUNK])
        u = _dot(hn, wi_ref[:, D_FF + lo:D_FF + lo + FF_CHUNK])
        a = (_silu(g) * u).astype(BF16)
        part = _dot(a, wo_ref[lo:lo + FF_CHUNK, :])
        if j == 0:
            acc_ref[...] = part
        else:
            acc_ref[...] += part
    y = x + 0.5 * gate * acc_ref[...]
    if final:
        y = _rms(y, fg_ref[...])
    o_ref[...] = y


def _ffn(x2d, m_all, m_index, g, wi, wo, k, tm, mix=None):
    n, d = x2d.shape
    final = mix is not None
    tok = lambda c: pl.BlockSpec((tm, c), lambda i: (i, 0))
    in_specs = [tok(d),
                pl.BlockSpec((1, N_MOD, d), lambda i: (m_index(i), 0, 0)),
                _const_spec((1, d)),
                _const_spec(wi.shape),
                _const_spec(wo.shape)]
    args = [x2d, m_all, g, wi, wo]
    if final:
        in_specs += [tok(a.shape[1]) for a in mix[:4]] + [_const_spec(a.shape) for a in mix[4:]]
        args += list(mix)
    return pl.pallas_call(
        functools.partial(_ffn_kernel, k=k, final=final),
        out_shape=jax.ShapeDtypeStruct((n, d), F32),
        grid=(n // tm,),
        in_specs=in_specs,
        out_specs=pl.BlockSpec((tm, d), lambda i: (i, 0)),
        scratch_shapes=[pltpu.VMEM((tm, d), F32)],
        compiler_params=_params(1),
        name="ffn_final" if final else "ffn",
    )(*args)


def _in_proj_kernel(x_ref, xp_ref, xn_ref, m_ref, g_ref, w_ref, cw_ref, cb_ref, kvn_ref, wuk_ref,
                    wuv_ref, tk_ref, *rest, with_q, n_alias):
    if with_q:
        qn_ref, wqa_ref, tqc_ref, tqs_ref = rest[:4]
        rest = rest[4:]
    rest = rest[n_alias:]
    u_ref, small_ref, k_ref, v_ref = rest[:4]
    if with_q:
        z_ref, q_ref = rest[4:6]
    ext_ref = rest[-1]
    m = m_ref[0]
    tm = x_ref.shape[1]
    i = pl.program_id(1)
    x_ext = jnp.concatenate([x_ref[0], xp_ref[0], xn_ref[0]], axis=0)
    hn_ext = (_rms(x_ext, g_ref[...]) * (1 + m[4:5]) + m[3:4]).astype(BF16)
    hn = hn_ext[0:tm]
    o_xbc = SSD_WIDTH
    o_cq = o_xbc + CONV_CH
    o_ckv = o_cq + Q_LORA
    o_small = o_ckv + KV_LORA

    xbc = _dot(hn_ext, w_ref[:, o_xbc:o_cq])
    has_prev = (i > 0).astype(F32)
    has_next = (i < pl.num_programs(1) - 1).astype(F32)
    ext_ref[0:SUBLANES] = xbc[tm:tm + SUBLANES] * has_prev
    ext_ref[SUBLANES:SUBLANES + tm] = xbc[0:tm]
    ext_ref[SUBLANES + tm:] = xbc[tm + SUBLANES:] * has_next
    conv = cb_ref[...]
    for tap in range(SSD_CONV):
        conv = conv + ext_ref[pl.ds(SUBLANES - SSD_CONV // 2 + tap, tm), :] * cw_ref[tap:tap + 1, :]
    u_ref[0] = _silu(conv).astype(BF16)

    small = _dot(hn, w_ref[:, o_small:o_small + SMALL_W])
    small_ref[0] = small
    ckv = _rms(_dot(hn, w_ref[:, o_ckv:o_small]), kvn_ref[...]).astype(BF16)
    t = small * tk_ref[...]
    lane = lax.broadcasted_iota(jnp.int32, (tm, LANES), 1)
    on_rope = jnp.logical_and(lane >= KPE_LANE0, lane < KPE_LANE0 + QK_ROPE)
    kpe = jnp.where(on_rope, t + pltpu.roll(t, QK_ROPE, axis=1), 0.0)
    k_nope = _dot(ckv, wuk_ref[...])
    for h in range(MLA_HEADS):
        hs = slice(h * HEAD_PAD, (h + 1) * HEAD_PAD)
        k_ref[0, :, hs] = (k_nope[:, hs] + kpe).astype(BF16)
    vt = _dot(ckv, wuv_ref[...]).T.astype(BF16)
    for n in range(tm // KEY_BLOCK):
        v_ref[0, n] = vt[:, n * KEY_BLOCK:(n + 1) * KEY_BLOCK]
    if with_q:
        z_ref[0] = _dot(hn, w_ref[:, 0:o_xbc])
        cq = _rms(_dot(hn, w_ref[:, o_cq:o_ckv]), qn_ref[...]).astype(BF16)
        qa = _dot(cq, wqa_ref[...])
        tqc, tqs = tqc_ref[...], tqs_ref[...]
        first_half = lax.broadcasted_iota(jnp.int32, (tm, HEAD_PAD), 1) % (QK_ROPE // 2) < QK_ROPE // 4
        for h in range(MLA_HEADS):
            hs = slice(h * HEAD_PAD, (h + 1) * HEAD_PAD)
            qh = qa[:, hs]
            partner = jnp.where(first_half, pltpu.roll(qh, HEAD_PAD - QK_ROPE // 4, axis=1),
                                pltpu.roll(qh, QK_ROPE // 4, axis=1))
            q_ref[0, :, hs] = (qh * tqc + partner * tqs).astype(BF16)


def _in_proj(h3d, m_all, m_index, g, w_all, cw, cb, kvn, wuk, wuv, tk, tm, row0, s_total,
             q_args=None, alias_bufs=()):
    b, rows, d = h3d.shape
    assert tm % KEY_BLOCK == 0 and row0 % tm == 0
    with_q = q_args is not None
    blk0 = row0 // tm
    hb, n8 = tm // SUBLANES, rows // SUBLANES
    tok = lambda c: pl.BlockSpec((1, tm, c), lambda bi, i: (bi, i, 0))
    comb = lambda c: pl.BlockSpec((1, tm, c), lambda bi, i: (bi, blk0 + i, 0))
    table = pl.BlockSpec((tm, LANES), lambda bi, i: (i, 0))
    in_specs = [tok(d),
                pl.BlockSpec((1, SUBLANES, d), lambda bi, i: (bi, jnp.maximum(i * hb - 1, 0), 0)),
                pl.BlockSpec((1, SUBLANES, d), lambda bi, i: (bi, jnp.minimum(i * hb + hb, n8 - 1), 0)),
                pl.BlockSpec((1, N_MOD, d), lambda bi, i: (m_index(bi), 0, 0)),
                _const_spec(g.shape), _const_spec(w_all.shape), _const_spec(cw.shape), _const_spec(cb.shape),
                _const_spec(kvn.shape), _const_spec(wuk.shape), _const_spec(wuv.shape), table]
    args = [h3d, h3d, h3d, m_all, g, w_all, cw, cb, kvn, wuk, wuv, tk]
    out_shape = [jax.ShapeDtypeStruct((b, s_total, CONV_CH), BF16),
                 jax.ShapeDtypeStruct((b, s_total, SMALL_W), F32),
                 jax.ShapeDtypeStruct((b, s_total, MLA_PAD), BF16),
                 jax.ShapeDtypeStruct((b, s_total // KEY_BLOCK, MLA_HEADS * V_DIM, KEY_BLOCK), BF16)]
    out_specs = [comb(CONV_CH), comb(SMALL_W), comb(MLA_PAD),
                 pl.BlockSpec((1, tm // KEY_BLOCK, MLA_HEADS * V_DIM, KEY_BLOCK),
                              lambda bi, i: (bi, blk0 + i, 0, 0))]
    if with_q:
        qn, wqa, tqc, tqs = q_args
        in_specs += [_const_spec(qn.shape), _const_spec(wqa.shape), table, table]
        args += [qn, wqa, tqc, tqs]
        out_shape += [jax.ShapeDtypeStruct((b, rows, SSD_WIDTH), F32),
                      jax.ShapeDtypeStruct((b, rows, MLA_PAD), BF16)]
        out_specs += [tok(SSD_WIDTH), tok(MLA_PAD)]
    aliases = {len(args) + j: j for j in range(len(alias_bufs))}
    in_specs += [pl.BlockSpec(memory_space=pl.ANY)] * len(alias_bufs)
    args += list(alias_bufs)
    return pl.pallas_call(
        functools.partial(_in_proj_kernel, with_q=with_q, n_alias=len(alias_bufs)),
        out_shape=out_shape,
        grid=(b, rows // tm),
        in_specs=in_specs,
        out_specs=out_specs,
        input_output_aliases=aliases,
        scratch_shapes=[pltpu.VMEM((tm + 2 * SUBLANES, CONV_CH), F32)],
        compiler_params=_params(2),
        name="in_proj_q" if with_q else "in_proj_ctx",
    )(*args)


def _ssd_kernel(uf_ref, sf_ref, ub_ref, sb_ref, dtb_ref, aneg_ref, dskip_ref, ex_ref, yf_ref, yb_ref,
                hf_ref, hb_ref, *, ncb):
    q = SSD_CHUNK
    t = pl.program_id(1)
    n_sub = uf_ref.shape[1] // q
    u_refs, small_refs, h_refs, y_refs = (uf_ref, ub_ref), (sf_ref, sb_ref), (hf_ref, hb_ref), (yf_ref, yb_ref)
    lanes = [(d, k if d == 0 else n_sub - 1 - k) for k in range(n_sub) for d in (0, 1)]
    rows = [slice(sub * q, (sub + 1) * q) for _, sub in lanes]
    ids = range(len(lanes))

    @pl.when(t == 0)
    def _():
        hf_ref[...] = jnp.zeros_like(hf_ref)
        hb_ref[...] = jnp.zeros_like(hb_ref)

    o_b = SSD_WIDTH
    o_c = SSD_WIDTH + SSD_GROUPS * SSD_STATE
    xs = [u_refs[d][0, rows[i], 0:o_b].astype(F32) for i, (d, _) in enumerate(lanes)]
    bm = [u_refs[d][0, rows[i], o_b:o_c] for i, (d, _) in enumerate(lanes)]
    cm = [u_refs[d][0, rows[i], o_c:CONV_CH] for i, (d, _) in enumerate(lanes)]

    ri = lax.broadcasted_iota(jnp.int32, (q, q), 0)
    ci = lax.broadcasted_iota(jnp.int32, (q, q), 1)
    keep = [ri >= ci, ri <= ci]
    last = [q - 1, 0]
    tri = [jnp.concatenate([jnp.where(keep[d], 1.0, 0.0).astype(BF16)] * 3, axis=1) for d in (0, 1)]
    dtv, parts = [], []
    for i, (d, _) in enumerate(lanes):
        pre = small_refs[d][0, rows[i], :] + dtb_ref[...]
        dtv.append(jnp.maximum(pre, 0.0) + jnp.log(1.0 + jnp.exp(-jnp.abs(pre))))
        a = dtv[i] * aneg_ref[...]
        a1 = a.astype(BF16)
        r1 = a - a1.astype(F32)
        a2 = r1.astype(BF16)
        a3 = (r1 - a2.astype(F32)).astype(BF16)
        parts.append(jnp.concatenate([a1, a2, a3], axis=0))
    acum = [_dot(tri[lanes[i][0]], parts[i]) for i in ids]
    acum_t = [acum[i].T for i in ids]
    dend = [jnp.exp(acum[i][last[d]:last[d] + 1, :] - acum[i]) for i, (d, _) in enumerate(lanes)]
    dstart = [jnp.exp(acum[i]) for i in ids]

    head_lane = lambda d, h: DT_LANE0 + SSD_HEADS * d + h

    def expand(arr, d):
        hi = arr.astype(BF16)
        lo = (arr - hi.astype(F32)).astype(BF16)
        return _dot(jnp.concatenate([hi, lo], axis=1), ex_ref[d])

    dstart_x = [expand(dstart[i], d) for i, (d, _) in enumerate(lanes)]
    xdt = [xs[i] * expand(dtv[i], d) for i, (d, _) in enumerate(lanes)]
    xend = [(xdt[i] * expand(dend[i], d)).astype(BF16) for i, (d, _) in enumerate(lanes)]
    gw = SSD_WIDTH // SSD_GROUPS
    hpg = SSD_HEADS // SSD_GROUPS
    glane = lax.broadcasted_iota(jnp.int32, (q, gw), 1)
    ys = [[] for _ in ids]
    for g in range(SSD_GROUPS):
        gs = slice(g * SSD_STATE, (g + 1) * SSD_STATE)
        gc = slice(g * gw, (g + 1) * gw)
        cbm = [_dot_nt(cm[i][:, gs], bm[i][:, gs]) for i in ids]
        yg = [None for _ in ids]
        for h0 in range(0, hpg, 2):
            for i, (d, _) in enumerate(lanes):
                lmats, xmasks = [], []
                for hh in (h0, h0 + 1):
                    hl = head_lane(d, g * hpg + hh)
                    seg = acum[i][:, hl:hl + 1] - acum_t[i][hl:hl + 1, :]
                    lmats.append((cbm[i] * jnp.exp(jnp.where(keep[d], seg, -jnp.inf))).astype(BF16))
                    in_head = jnp.logical_and(glane >= hh * SSD_HEAD_DIM, glane < (hh + 1) * SSD_HEAD_DIM)
                    xmasks.append(jnp.where(in_head, xdt[i][:, gc], 0.0).astype(BF16))
                part = _dot(jnp.concatenate(lmats, axis=1), jnp.concatenate(xmasks, axis=0))
                yg[i] = part if yg[i] is None else yg[i] + part
        for i, (d, _) in enumerate(lanes):
            hst = h_refs[d][g]
            yoff = _dot(cm[i][:, gs], hst.astype(BF16)) * dstart_x[i][:, gc]
            ys[i].append(yg[i] + yoff)
            h_refs[d][g] = (dstart_x[i][last[d]:last[d] + 1, gc] * hst
                            + _dot_tn(bm[i][:, gs], xend[i][:, gc]))

    @pl.when(t >= ncb)
    def _():
        for i, (d, _) in enumerate(lanes):
            y = jnp.concatenate(ys[i], axis=1)
            if d == 0:
                y = y + dskip_ref[...] * xs[i]
            y_refs[d][0, rows[i], :] = y


def _ssd(u_all, small_all, dtb, aneg, dskip, l_x, l_c):
    b = u_all.shape[0]
    n_sub = SSD_BLOCK_CHUNKS if (l_x % (SSD_BLOCK_CHUNKS * SSD_CHUNK) == 0
                                 and l_c % (SSD_BLOCK_CHUNKS * SSD_CHUNK) == 0) else 1
    rb = n_sub * SSD_CHUNK
    nxb, ncb = l_x // rb, l_c // rb
    fwd = lambda t: jnp.where(t < ncb, nxb + t, t - ncb)
    bwd = lambda t: nxb + ncb - 1 - t
    block_spec = lambda w, order: pl.BlockSpec((1, rb, w), lambda bi, t: (bi, order(t), 0))
    y_spec = lambda order: pl.BlockSpec((1, rb, SSD_WIDTH), lambda bi, t: (bi, order(jnp.maximum(t, ncb)), 0))
    y_shape = jax.ShapeDtypeStruct((b, l_x, SSD_WIDTH), F32)
    h_buf = pltpu.VMEM((SSD_GROUPS, SSD_STATE, SSD_WIDTH // SSD_GROUPS), F32)
    ex_np = np.zeros((N_DIR, SMALL_W, SSD_WIDTH), np.float32)
    for dd in range(N_DIR):
        for h in range(SSD_HEADS):
            ex_np[dd, DT_LANE0 + SSD_HEADS * dd + h, h * SSD_HEAD_DIM:(h + 1) * SSD_HEAD_DIM] = 1.0
    ex = jnp.asarray(np.concatenate([ex_np, ex_np], axis=1), BF16)
    return pl.pallas_call(
        functools.partial(_ssd_kernel, ncb=ncb),
        out_shape=[y_shape, y_shape],
        grid=(b, nxb + ncb),
        in_specs=[block_spec(CONV_CH, fwd), block_spec(SMALL_W, fwd),
                  block_spec(CONV_CH, bwd), block_spec(SMALL_W, bwd),
                  _const_spec(dtb.shape), _const_spec(aneg.shape), _const_spec(dskip.shape),
                  _const_spec(ex.shape)],
        out_specs=[y_spec(fwd), y_spec(bwd)],
        scratch_shapes=[h_buf, h_buf],
        compiler_params=_params(2),
        name="ssd",
    )(u_all, small_all, u_all, small_all, dtb, aneg, dskip, ex)


def _attn_kernel(q_ref, k_ref, vt_ref, o_ref):
    nb = vt_ref.shape[1]
    kb = vt_ref.shape[3]
    heads = range(ATTN_HEADS)
    ones = jnp.ones((2 * SUBLANES, kb), BF16)
    q = [q_ref[0, :, hh * HEAD_PAD:(hh + 1) * HEAD_PAD] for hh in heads]

    def scores(j):
        return [_dot_nt(k_ref[0, j * kb:(j + 1) * kb, hh * HEAD_PAD:(hh + 1) * HEAD_PAD], q[hh])
                for hh in heads]

    m = [None] * ATTN_HEADS
    acc = [None] * ATTN_HEADS
    pending = [scores(j) for j in range(min(ATTN_LOOKAHEAD, nb))]
    for j in range(nb):
        s = pending.pop(0)
        if j + ATTN_LOOKAHEAD < nb:
            pending.append(scores(j + ATTN_LOOKAHEAD))
        for hh in heads:
            vt = jnp.concatenate([vt_ref[0, j, hh * V_DIM:(hh + 1) * V_DIM, :], ones], axis=0)
            m_blk = jnp.max(s[hh], axis=0, keepdims=True)
            m_new = m_blk if j == 0 else jnp.maximum(m[hh], m_blk)
            part = _dot(vt, jnp.exp2(s[hh] - m_new).astype(BF16))
            acc[hh] = part if j == 0 else acc[hh] * jnp.exp2(m[hh] - m_new) + part
            m[hh] = m_new

    outs = [(acc[hh][0:V_DIM] / acc[hh][V_DIM:V_DIM + 1]).T for hh in heads]
    o_ref[0] = jnp.concatenate(outs, axis=1).astype(BF16)


def _attn(q, k_all, vt_all, tq):
    b, l_x, _ = q.shape
    s_total = k_all.shape[1]
    nb, kb = vt_all.shape[1], vt_all.shape[3]
    hw = ATTN_HEADS * HEAD_PAD
    return pl.pallas_call(
        _attn_kernel,
        out_shape=jax.ShapeDtypeStruct((b, l_x, MLA_HEADS * V_DIM), BF16),
        grid=(b, MLA_HEADS // ATTN_HEADS, l_x // tq),
        in_specs=[pl.BlockSpec((1, tq, hw), lambda bi, h, i: (bi, i, h)),
                  pl.BlockSpec((1, s_total, hw), lambda bi, h, i: (bi, 0, h)),
                  pl.BlockSpec((1, nb, ATTN_HEADS * V_DIM, kb), lambda bi, h, i: (bi, 0, h, 0))],
        out_specs=pl.BlockSpec((1, tq, ATTN_HEADS * V_DIM), lambda bi, h, i: (bi, i, h)),
        compiler_params=_params(3),
        name="attn",
    )(q, k_all, vt_all)


def _rope_swap(w):
    parts = []
    for base in (0, QK_ROPE // 2):
        u1 = w[..., base:base + QK_ROPE // 4]
        u2 = w[..., base + QK_ROPE // 4:base + QK_ROPE // 2]
        parts += [-u2, u1]
    return jnp.concatenate(parts, axis=-1)


def _rope_tables(l_x):
    half = QK_ROPE // 2
    inv = (ROPE_BASE ** (-np.arange(0, half, 2, dtype=np.float32) / half)).astype(np.float32)
    rows = l_x // GRID_W
    row = np.repeat(np.arange(rows), GRID_W).astype(np.float32)
    col = (np.arange(rows * GRID_W) % GRID_W).astype(np.float32)
    ar = (row[:, None] * inv).astype(np.float64)
    ac = (col[:, None] * inv).astype(np.float64)
    cos = np.concatenate([np.cos(ar), np.cos(ar), np.cos(ac), np.cos(ac)], axis=1).astype(np.float32)
    sin = np.concatenate([np.sin(ar), np.sin(ar), np.sin(ac), np.sin(ac)], axis=1).astype(np.float32)
    return cos, sin


def kernel(x, c, ctx, c_ctx, w_ada, b_ada, norm_ffn1, w_ffn1_in, w_ffn1_out, norm_mix, w_in, conv_w, conv_b, dt_bias, a_log, d_skip, ssd_norm, q_norm, w_uq, kv_norm, w_ukv, w_out, norm_ffn2, w_ffn2_in, w_ffn2_out, final_norm):
    b, l_x, d = x.shape
    l_c = ctx.shape[1]
    assert d == D_MODEL and w_ada.shape[0] == 1
    assert l_x % KEY_BLOCK == 0 and l_c % KEY_BLOCK == 0 and l_x % GRID_W == 0
    s_total = l_x + l_c

    c_rows = jnp.concatenate([c, c_ctx[None, :]], axis=0)
    pad = (-c_rows.shape[0]) % SUBLANES
    c_rows = jnp.pad(c_rows, ((0, pad), (0, 0)))
    m_all = _ada(c_rows, w_ada[0], b_ada)[:b + 1].reshape(b + 1, N_MOD, d)

    tm = _pick_tile(l_x, FFN_TILE)
    tmc = _pick_tile(b * l_c, FFN_TILE)
    x_tiles = l_x // tm
    m_lat = lambda i: i // x_tiles
    m_ctx = lambda i: b

    wi1, wo1 = w_ffn1_in[0].astype(BF16), w_ffn1_out[0].astype(BF16)
    x1 = _ffn(x.reshape(b * l_x, d), m_all, m_lat, norm_ffn1, wi1, wo1, 0, tm)
    c1 = _ffn(ctx.reshape(b * l_c, d), m_all, m_ctx, norm_ffn1, wi1, wo1, 0, tmc)

    w = w_in[0]
    w_kpe = w[:, SPLIT_CKV:]
    n_dt = N_DIR * SSD_HEADS
    small = jnp.concatenate([w[:, SPLIT_XBC:SPLIT_DT], jnp.zeros((d, KPE_LANE0 - QK_ROPE - n_dt), F32),
                             _rope_swap(w_kpe), w_kpe,
                             jnp.zeros((d, SMALL_W - KPE_LANE0 - QK_ROPE), F32)], axis=1)
    w_all = jnp.concatenate([w[:, :SPLIT_XBC], w[:, SPLIT_DT:SPLIT_CKV], small], axis=1).astype(BF16)
    hq = QK_NOPE + QK_ROPE
    wq = w_uq[0].reshape(Q_LORA, MLA_HEADS, hq)
    zq = jnp.zeros((Q_LORA, MLA_HEADS, HEAD_PAD - hq), F32)
    wqa = jnp.concatenate([wq, zq], axis=2).reshape(Q_LORA, MLA_PAD).astype(BF16)
    wkv = w_ukv[0].reshape(KV_LORA, MLA_HEADS, QK_NOPE + V_DIM)
    wuk = jnp.concatenate([wkv[:, :, :QK_NOPE], jnp.zeros((KV_LORA, MLA_HEADS, HEAD_PAD - QK_NOPE), F32)],
                          axis=2).reshape(KV_LORA, MLA_PAD).astype(BF16)
    wuv = wkv[:, :, QK_NOPE:].reshape(KV_LORA, MLA_HEADS * V_DIM).astype(BF16)
    cos, sin = _rope_tables(l_x)
    zpad = lambda n: np.zeros((l_x, n), np.float32)
    q_scale = np.float32(MLA_SCALE * math.log2(math.e))
    tqc = jnp.asarray(q_scale * np.concatenate([np.ones((l_x, QK_NOPE), np.float32), cos, zpad(HEAD_PAD - hq)], axis=1))
    sign = np.tile(np.repeat(np.float32([-1.0, 1.0]), QK_ROPE // 4), 2)
    tqs = jnp.asarray(q_scale * np.concatenate([zpad(QK_NOPE), sin * sign, zpad(HEAD_PAD - hq)], axis=1))
    tk_lat = jnp.asarray(np.concatenate([zpad(KPE_LANE0 - QK_ROPE), sin, cos,
                                         zpad(SMALL_W - KPE_LANE0 - QK_ROPE)], axis=1))
    tk_ctx_np = np.zeros((l_c, SMALL_W), np.float32)
    tk_ctx_np[:, KPE_LANE0:KPE_LANE0 + QK_ROPE] = 1.0
    tk_ctx = jnp.asarray(tk_ctx_np)

    cw = jnp.pad(conv_w[0], ((0, SUBLANES - SSD_CONV), (0, 0)))
    shared = (norm_mix, w_all, cw, conv_b, kv_norm, wuk, wuv)
    u_all, small_all, k_all, v_all, z, q = _in_proj(
        x1.reshape(b, l_x, d), m_all, lambda bi: bi, *shared, tk_lat, _pick_tile(l_x, PROJ_TILE), 0, s_total,
        q_args=(q_norm, wqa, tqc, tqs))
    u_all, small_all, k_all, v_all = _in_proj(
        c1.reshape(b, l_c, d), m_all, lambda bi: b, *shared, tk_ctx,
        _pick_tile(math.gcd(l_c, l_x), PROJ_CTX_TILE), l_x, s_total,
        alias_bufs=(u_all, small_all, k_all, v_all))

    lane_pad = lambda v: jnp.pad(v.reshape(1, -1), ((0, 0), (DT_LANE0, SMALL_W - DT_LANE0 - v.size)))
    dtb = lane_pad(dt_bias[0])
    aneg = lane_pad(-jnp.exp(a_log[0].astype(F32)))
    dskip = jnp.repeat(d_skip[0], SSD_HEAD_DIM)[None, :]
    yf, yb = _ssd(u_all, small_all, dtb, aneg, dskip, l_x, l_c)

    mla = _attn(q, k_all, v_all, _pick_tile(l_x, ATTN_Q_TILE))

    wo = w_out[0]
    ws = wo[:SSD_WIDTH].astype(BF16)
    wm = wo[SSD_WIDTH:].astype(BF16)
    n = b * l_x
    mix = (yf.reshape(n, SSD_WIDTH), yb.reshape(n, SSD_WIDTH), z.reshape(n, SSD_WIDTH),
           mla.reshape(n, MLA_HEADS * V_DIM), ssd_norm, ws, wm, final_norm[None, :])
    out = _ffn(x1, m_all, m_lat, norm_ffn2, w_ffn2_in[0].astype(BF16), w_ffn2_out[0].astype(BF16), 6, tm,
               mix=mix)
    return out.reshape(b, l_x, d)
```

```python
import functools
import math

import jax
import jax.numpy as jnp
import numpy as np
from jax import lax
from jax.experimental import pallas as pl
from jax.experimental.pallas import tpu as pltpu

F32 = jnp.float32
BF16 = jnp.bfloat16

D_MODEL = 1024
GRID_W = 64
N_MOD = 9
D_FF = 2816
EPS = 1e-6
SSD_HEADS = 8
SSD_HEAD_DIM = 64
SSD_WIDTH = SSD_HEADS * SSD_HEAD_DIM
SSD_GROUPS = 2
SSD_STATE = 128
SSD_CONV = 5
SSD_CHUNK = 128
N_DIR = 2
CONV_CH = SSD_WIDTH + 2 * SSD_GROUPS * SSD_STATE
MLA_HEADS = 8
QK_NOPE = 64
QK_ROPE = 32
V_DIM = 64
Q_LORA = 768
KV_LORA = 256
MLA_SCALE = (QK_NOPE + QK_ROPE) ** -0.5
ROPE_BASE = 10000.0
SPLIT_XBC = SSD_WIDTH + CONV_CH
SPLIT_DT = SPLIT_XBC + N_DIR * SSD_HEADS
SPLIT_CQ = SPLIT_DT + Q_LORA
SPLIT_CKV = SPLIT_CQ + KV_LORA

LANES = 128
SUBLANES = 8
VMEM_LIMIT = 56 * 1024 * 1024

HEAD_PAD = LANES
MLA_PAD = MLA_HEADS * HEAD_PAD
SMALL_W = LANES
DT_LANE0 = 0
KPE_LANE0 = QK_NOPE
KEY_BLOCK = 256

ADA_COLS = 1024
FFN_TILE = 512
FF_CHUNK = 256
PROJ_TILE = 1024
PROJ_CTX_TILE = 256
SSD_BLOCK_CHUNKS = 2
ATTN_Q_TILE = 1024
ATTN_HEADS = LANES // V_DIM
ATTN_LOOKAHEAD = 2


def _dot(a, b):
    return jnp.dot(a, b, preferred_element_type=F32)


def _dot_nt(a, b):
    return lax.dot_general(a, b, (((1,), (1,)), ((), ())), preferred_element_type=F32)


def _dot_tn(a, b):
    return lax.dot_general(a, b, (((0,), (0,)), ((), ())), preferred_element_type=F32)


def _silu(x):
    return x * (1.0 / (1.0 + jnp.exp(-x)))


def _rms(x, g):
    return x * lax.rsqrt(jnp.mean(x * x, axis=-1, keepdims=True) + EPS) * g


def _const_spec(shape):
    zeros = (0,) * len(shape)
    return pl.BlockSpec(shape, lambda *_: zeros, pipeline_mode=pl.Buffered(1))


def _params(n_axes):
    return pltpu.CompilerParams(dimension_semantics=("arbitrary",) * n_axes,
                                vmem_limit_bytes=VMEM_LIMIT)


def _pick_tile(n, pref):
    t = min(pref, n)
    while n % t:
        t //= 2
    return t


def _ada_kernel(c_ref, w_ref, b_ref, o_ref):
    s = _silu(c_ref[...]).astype(BF16)
    o_ref[...] = _dot(s, w_ref[...].astype(BF16)) + b_ref[...]


def _ada(c_rows, w, b):
    rows, d = c_rows.shape
    n = w.shape[1]
    tn = _pick_tile(n, ADA_COLS)
    return pl.pallas_call(
        _ada_kernel,
        out_shape=jax.ShapeDtypeStruct((rows, n), F32),
        grid=(n // tn,),
        in_specs=[pl.BlockSpec((rows, d), lambda j: (0, 0)),
                  pl.BlockSpec((d, tn), lambda j: (0, j)),
                  pl.BlockSpec((1, tn), lambda j: (0, j))],
        out_specs=pl.BlockSpec((rows, tn), lambda j: (0, j)),
        compiler_params=_params(1),
        name="ada",
    )(c_rows, w, b)


def _ffn_kernel(x_ref, m_ref, g_ref, wi_ref, wo_ref, *rest, k, final):
    x = x_ref[...]
    m = m_ref[0]
    if final:
        yf_ref, yb_ref, z_ref, a_ref, sn_ref, ws_ref, wm_ref, fg_ref, o_ref, acc_ref = rest
        y = (yf_ref[...] + yb_ref[...]) * _silu(z_ref[...])
        ssd = _rms(y, sn_ref[...]).astype(BF16)
        x = x + m[k - 1:k] * (_dot(ssd, ws_ref[...]) + _dot(a_ref[...], wm_ref[...]))
    else:
        o_ref, acc_ref = rest
    shift, scale, gate = m[k:k + 1], m[k + 1:k + 2], m[k + 2:k + 3]
    hn = (_rms(x, g_ref[...]) * (1 + scale) + shift).astype(BF16)
    for j in range(D_FF // FF_CHUNK):
        lo = j * FF_CHUNK
        g = _dot(hn, wi_ref[:, lo:lo + FF_CHUNK])
        u = _dot(hn, wi_ref[:, D_FF + lo:D_FF + lo + FF_CHUNK])
        a = (_silu(g) * u).astype(BF16)
        part = _dot(a, wo_ref[lo:lo + FF_CHUNK, :])
        if j == 0:
            acc_ref[...] = part
        else:
            acc_ref[...] += part
    y = x + 0.5 * gate * acc_ref[...]
    if final:
        y = _rms(y, fg_ref[...])
    o_ref[...] = y


def _ffn(x2d, m_all, m_index, g, wi, wo, k, tm, mix=None):
    n, d = x2d.shape
    final = mix is not None
    tok = lambda c: pl.BlockSpec((tm, c), lambda i: (i, 0))
    in_specs = [tok(d),
                pl.BlockSpec((1, N_MOD, d), lambda i: (m_index(i), 0, 0)),
                _const_spec((1, d)),
                _const_spec(wi.shape),
                _const_spec(wo.shape)]
    args = [x2d, m_all, g, wi, wo]
    if final:
        in_specs += [tok(a.shape[1]) for a in mix[:4]] + [_const_spec(a.shape) for a in mix[4:]]
        args += list(mix)
    return pl.pallas_call(
        functools.partial(_ffn_kernel, k=k, final=final),
        out_shape=jax.ShapeDtypeStruct((n, d), F32),
        grid=(n // tm,),
        in_specs=in_specs,
        out_specs=pl.BlockSpec((tm, d), lambda i: (i, 0)),
        scratch_shapes=[pltpu.VMEM((tm, d), F32)],
        compiler_params=_params(1),
        name="ffn_final" if final else "ffn",
    )(*args)


def _in_proj_kernel(x_ref, xp_ref, xn_ref, m_ref, g_ref, w_ref, cw_ref, cb_ref, kvn_ref, wuk_ref,
                    wuv_ref, tk_ref, *rest, with_q, n_alias):
    if with_q:
        qn_ref, wqa_ref, tqc_ref, tqs_ref = rest[:4]
        rest = rest[4:]
    rest = rest[n_alias:]
    u_ref, small_ref, k_ref, v_ref = rest[:4]
    if with_q:
        z_ref, q_ref = rest[4:6]
    ext_ref = rest[-1]
    m = m_ref[0]
    tm = x_ref.shape[1]
    i = pl.program_id(1)
    x_ext = jnp.concatenate([x_ref[0], xp_ref[0], xn_ref[0]], axis=0)
    hn_ext = (_rms(x_ext, g_ref[...]) * (1 + m[4:5]) + m[3:4]).astype(BF16)
    hn = hn_ext[0:tm]
    o_xbc = SSD_WIDTH
    o_cq = o_xbc + CONV_CH
    o_ckv = o_cq + Q_LORA
    o_small = o_ckv + KV_LORA

    xbc = _dot(hn_ext, w_ref[:, o_xbc:o_cq])
    if with_q:
        z_ref[0] = _dot(hn, w_ref[:, 0:o_xbc])
        cq_raw = _dot(hn, w_ref[:, o_cq:o_ckv])
    ckv_raw = _dot(hn, w_ref[:, o_ckv:o_small])
    small = _dot(hn, w_ref[:, o_small:o_small + SMALL_W])
    small_ref[0] = small

    has_prev = (i > 0).astype(F32)
    has_next = (i < pl.num_programs(1) - 1).astype(F32)
    ext_ref[0:SUBLANES] = xbc[tm:tm + SUBLANES] * has_prev
    ext_ref[SUBLANES:SUBLANES + tm] = xbc[0:tm]
    ext_ref[SUBLANES + tm:] = xbc[tm + SUBLANES:] * has_next
    conv = cb_ref[...]
    for tap in range(SSD_CONV):
        conv = conv + ext_ref[pl.ds(SUBLANES - SSD_CONV // 2 + tap, tm), :] * cw_ref[tap:tap + 1, :]
    u_ref[0] = _silu(conv).astype(BF16)

    ckv = _rms(ckv_raw, kvn_ref[...]).astype(BF16)
    t = small * tk_ref[...]
    lane = lax.broadcasted_iota(jnp.int32, (tm, LANES), 1)
    on_rope = jnp.logical_and(lane >= KPE_LANE0, lane < KPE_LANE0 + QK_ROPE)
    kpe = jnp.where(on_rope, t + pltpu.roll(t, QK_ROPE, axis=1), 0.0)
    k_nope = _dot(ckv, wuk_ref[...])
    for h in range(MLA_HEADS):
        hs = slice(h * HEAD_PAD, (h + 1) * HEAD_PAD)
        k_ref[0, :, hs] = (k_nope[:, hs] + kpe).astype(BF16)
    vt = _dot(ckv, wuv_ref[...]).T.astype(BF16)
    for n in range(tm // KEY_BLOCK):
        v_ref[0, n] = vt[:, n * KEY_BLOCK:(n + 1) * KEY_BLOCK]
    if with_q:
        cq = _rms(cq_raw, qn_ref[...]).astype(BF16)
        qa = _dot(cq, wqa_ref[...])
        tqc, tqs = tqc_ref[...], tqs_ref[...]
        first_half = lax.broadcasted_iota(jnp.int32, (tm, HEAD_PAD), 1) % (QK_ROPE // 2) < QK_ROPE // 4
        for h in range(MLA_HEADS):
            hs = slice(h * HEAD_PAD, (h + 1) * HEAD_PAD)
            qh = qa[:, hs]
            partner = jnp.where(first_half, pltpu.roll(qh, HEAD_PAD - QK_ROPE // 4, axis=1),
                                pltpu.roll(qh, QK_ROPE // 4, axis=1))
            q_ref[0, :, hs] = (qh * tqc + partner * tqs).astype(BF16)


def _in_proj(h3d, m_all, m_index, g, w_all, cw, cb, kvn, wuk, wuv, tk, tm, row0, s_total,
             q_args=None, alias_bufs=()):
    b, rows, d = h3d.shape
    assert tm % KEY_BLOCK == 0 and row0 % tm == 0
    with_q = q_args is not None
    blk0 = row0 // tm
    hb, n8 = tm // SUBLANES, rows // SUBLANES
    tok = lambda c: pl.BlockSpec((1, tm, c), lambda bi, i: (bi, i, 0))
    comb = lambda c: pl.BlockSpec((1, tm, c), lambda bi, i: (bi, blk0 + i, 0))
    table = pl.BlockSpec((tm, LANES), lambda bi, i: (i, 0))
    in_specs = [tok(d),
                pl.BlockSpec((1, SUBLANES, d), lambda bi, i: (bi, jnp.maximum(i * hb - 1, 0), 0)),
                pl.BlockSpec((1, SUBLANES, d), lambda bi, i: (bi, jnp.minimum(i * hb + hb, n8 - 1), 0)),
                pl.BlockSpec((1, N_MOD, d), lambda bi, i: (m_index(bi), 0, 0)),
                _const_spec(g.shape), _const_spec(w_all.shape), _const_spec(cw.shape), _const_spec(cb.shape),
                _const_spec(kvn.shape), _const_spec(wuk.shape), _const_spec(wuv.shape), table]
    args = [h3d, h3d, h3d, m_all, g, w_all, cw, cb, kvn, wuk, wuv, tk]
    out_shape = [jax.ShapeDtypeStruct((b, s_total, CONV_CH), BF16),
                 jax.ShapeDtypeStruct((b, s_total, SMALL_W), F32),
                 jax.ShapeDtypeStruct((b, s_total, MLA_PAD), BF16),
                 jax.ShapeDtypeStruct((b, s_total // KEY_BLOCK, MLA_HEADS * V_DIM, KEY_BLOCK), BF16)]
    out_specs = [comb(CONV_CH), comb(SMALL_W), comb(MLA_PAD),
                 pl.BlockSpec((1, tm // KEY_BLOCK, MLA_HEADS * V_DIM, KEY_BLOCK),
                              lambda bi, i: (bi, blk0 + i, 0, 0))]
    if with_q:
        qn, wqa, tqc, tqs = q_args
        in_specs += [_const_spec(qn.shape), _const_spec(wqa.shape), table, table]
        args += [qn, wqa, tqc, tqs]
        out_shape += [jax.ShapeDtypeStruct((b, rows, SSD_WIDTH), F32),
                      jax.ShapeDtypeStruct((b, rows, MLA_PAD), BF16)]
        out_specs += [tok(SSD_WIDTH), tok(MLA_PAD)]
    aliases = {len(args) + j: j for j in range(len(alias_bufs))}
    in_specs += [pl.BlockSpec(memory_space=pl.ANY)] * len(alias_bufs)
    args += list(alias_bufs)
    return pl.pallas_call(
        functools.partial(_in_proj_kernel, with_q=with_q, n_alias=len(alias_bufs)),
        out_shape=out_shape,
        grid=(b, rows // tm),
        in_specs=in_specs,
        out_specs=out_specs,
        input_output_aliases=aliases,
        scratch_shapes=[pltpu.VMEM((tm + 2 * SUBLANES, CONV_CH), F32)],
        compiler_params=_params(2),
        name="in_proj_q" if with_q else "in_proj_ctx",
    )(*args)


def _ssd_kernel(uf_ref, sf_ref, ub_ref, sb_ref, dtb_ref, aneg_ref, dskip_ref, ex_ref, yf_ref, yb_ref,
                hf_ref, hb_ref, *, ncb):
    q = SSD_CHUNK
    t = pl.program_id(1)
    n_sub = uf_ref.shape[1] // q
    u_refs, small_refs, h_refs, y_refs = (uf_ref, ub_ref), (sf_ref, sb_ref), (hf_ref, hb_ref), (yf_ref, yb_ref)
    lanes = [(d, k if d == 0 else n_sub - 1 - k) for k in range(n_sub) for d in (0, 1)]
    rows = [slice(sub * q, (sub + 1) * q) for _, sub in lanes]
    ids = range(len(lanes))

    @pl.when(t == 0)
    def _():
        hf_ref[...] = jnp.zeros_like(hf_ref)
        hb_ref[...] = jnp.zeros_like(hb_ref)

    o_b = SSD_WIDTH
    o_c = SSD_WIDTH + SSD_GROUPS * SSD_STATE
    xs = [u_refs[d][0, rows[i], 0:o_b].astype(F32) for i, (d, _) in enumerate(lanes)]
    bm = [u_refs[d][0, rows[i], o_b:o_c] for i, (d, _) in enumerate(lanes)]
    cm = [u_refs[d][0, rows[i], o_c:CONV_CH] for i, (d, _) in enumerate(lanes)]
    gslice = [slice(g * SSD_STATE, (g + 1) * SSD_STATE) for g in range(SSD_GROUPS)]
    cbm = [[_dot_nt(cm[i][:, gs], bm[i][:, gs]) for i in ids] for gs in gslice]

    ri = lax.broadcasted_iota(jnp.int32, (q, q), 0)
    ci = lax.broadcasted_iota(jnp.int32, (q, q), 1)
    keep = [ri >= ci, ri <= ci]
    last = [q - 1, 0]
    tri = [jnp.concatenate([jnp.where(keep[d], 1.0, 0.0).astype(BF16)] * 3, axis=1) for d in (0, 1)]
    dtv, parts = [], []
    for i, (d, _) in enumerate(lanes):
        pre = small_refs[d][0, rows[i], :] + dtb_ref[...]
        dtv.append(jnp.maximum(pre, 0.0) + jnp.log(1.0 + jnp.exp(-jnp.abs(pre))))
        a = dtv[i] * aneg_ref[...]
        a1 = a.astype(BF16)
        r1 = a - a1.astype(F32)
        a2 = r1.astype(BF16)
        a3 = (r1 - a2.astype(F32)).astype(BF16)
        parts.append(jnp.concatenate([a1, a2, a3], axis=0))
    acum = [_dot(tri[lanes[i][0]], parts[i]) for i in ids]
    acum_t = [acum[i].T for i in ids]
    dend = [jnp.exp(acum[i][last[d]:last[d] + 1, :] - acum[i]) for i, (d, _) in enumerate(lanes)]
    dstart = [jnp.exp(acum[i]) for i in ids]

    head_lane = lambda d, h: DT_LANE0 + SSD_HEADS * d + h

    def expand(arr, d):
        hi = arr.astype(BF16)
        lo = (arr - hi.astype(F32)).astype(BF16)
        return _dot(jnp.concatenate([hi, lo], axis=1), ex_ref[d])

    dstart_x = [expand(dstart[i], d) for i, (d, _) in enumerate(lanes)]
    xdt = [xs[i] * expand(dtv[i], d) for i, (d, _) in enumerate(lanes)]
    xend = [(xdt[i] * expand(dend[i], d)).astype(BF16) for i, (d, _) in enumerate(lanes)]
    gw = SSD_WIDTH // SSD_GROUPS
    hpg = SSD_HEADS // SSD_GROUPS
    glane = lax.broadcasted_iota(jnp.int32, (q, gw), 1)
    ys = [[] for _ in ids]
    for g in range(SSD_GROUPS):
        gs = gslice[g]
        gc = slice(g * gw, (g + 1) * gw)
        yg = [None for _ in ids]
        for h0 in range(0, hpg, 2):
            for i, (d, _) in enumerate(lanes):
                lmats, xmasks = [], []
                for hh in (h0, h0 + 1):
                    hl = head_lane(d, g * hpg + hh)
                    seg = acum[i][:, hl:hl + 1] - acum_t[i][hl:hl + 1, :]
                    lmats.append((cbm[g][i] * jnp.exp(jnp.where(keep[d], seg, -jnp.inf))).astype(BF16))
                    in_head = jnp.logical_and(glane >= hh * SSD_HEAD_DIM, glane < (hh + 1) * SSD_HEAD_DIM)
                    xmasks.append(jnp.where(in_head, xdt[i][:, gc], 0.0).astype(BF16))
                part = _dot(jnp.concatenate(lmats, axis=1), jnp.concatenate(xmasks, axis=0))
                yg[i] = part if yg[i] is None else yg[i] + part
        for i, (d, _) in enumerate(lanes):
            hst = h_refs[d][g]
            yoff = _dot(cm[i][:, gs], hst.astype(BF16)) * dstart_x[i][:, gc]
            ys[i].append(yg[i] + yoff)
            h_refs[d][g] = (dstart_x[i][last[d]:last[d] + 1, gc] * hst
                            + _dot_tn(bm[i][:, gs], xend[i][:, gc]))

    @pl.when(t >= ncb)
    def _():
        for i, (d, _) in enumerate(lanes):
            y = jnp.concatenate(ys[i], axis=1)
            if d == 0:
                y = y + dskip_ref[...] * xs[i]
            y_refs[d][0, rows[i], :] = y


def _ssd(u_all, small_all, dtb, aneg, dskip, l_x, l_c):
    b = u_all.shape[0]
    n_sub = SSD_BLOCK_CHUNKS if (l_x % (SSD_BLOCK_CHUNKS * SSD_CHUNK) == 0
                                 and l_c % (SSD_BLOCK_CHUNKS * SSD_CHUNK) == 0) else 1
    rb = n_sub * SSD_CHUNK
    nxb, ncb = l_x // rb, l_c // rb
    fwd = lambda t: jnp.where(t < ncb, nxb + t, t - ncb)
    bwd = lambda t: nxb + ncb - 1 - t
    block_spec = lambda w, order: pl.BlockSpec((1, rb, w), lambda bi, t: (bi, order(t), 0))
    y_spec = lambda order: pl.BlockSpec((1, rb, SSD_WIDTH), lambda bi, t: (bi, order(jnp.maximum(t, ncb)), 0))
    y_shape = jax.ShapeDtypeStruct((b, l_x, SSD_WIDTH), F32)
    h_buf = pltpu.VMEM((SSD_GROUPS, SSD_STATE, SSD_WIDTH // SSD_GROUPS), F32)
    ex_np = np.zeros((N_DIR, SMALL_W, SSD_WIDTH), np.float32)
    for dd in range(N_DIR):
        for h in range(SSD_HEADS):
            ex_np[dd, DT_LANE0 + SSD_HEADS * dd + h, h * SSD_HEAD_DIM:(h + 1) * SSD_HEAD_DIM] = 1.0
    ex = jnp.asarray(np.concatenate([ex_np, ex_np], axis=1), BF16)
    return pl.pallas_call(
        functools.partial(_ssd_kernel, ncb=ncb),
        out_shape=[y_shape, y_shape],
        grid=(b, nxb + ncb),
        in_specs=[block_spec(CONV_CH, fwd), block_spec(SMALL_W, fwd),
                  block_spec(CONV_CH, bwd), block_spec(SMALL_W, bwd),
                  _const_spec(dtb.shape), _const_spec(aneg.shape), _const_spec(dskip.shape),
                  _const_spec(ex.shape)],
        out_specs=[y_spec(fwd), y_spec(bwd)],
        scratch_shapes=[h_buf, h_buf],
        compiler_params=_params(2),
        name="ssd",
    )(u_all, small_all, u_all, small_all, dtb, aneg, dskip, ex)


def _attn_kernel(q_ref, k_ref, vt_ref, o_ref):
    nb = vt_ref.shape[1]
    kb = vt_ref.shape[3]
    heads = range(ATTN_HEADS)
    ones = jnp.ones((2 * SUBLANES, kb), BF16)
    q = [q_ref[0, :, hh * HEAD_PAD:(hh + 1) * HEAD_PAD] for hh in heads]

    def scores(j):
        return [_dot_nt(k_ref[0, j * kb:(j + 1) * kb, hh * HEAD_PAD:(hh + 1) * HEAD_PAD], q[hh])
                for hh in heads]

    m = [None] * ATTN_HEADS
    acc = [None] * ATTN_HEADS
    pending = [scores(j) for j in range(min(ATTN_LOOKAHEAD, nb))]
    for j in range(nb):
        s = pending.pop(0)
        if j + ATTN_LOOKAHEAD < nb:
            pending.append(scores(j + ATTN_LOOKAHEAD))
        for hh in heads:
            vt = jnp.concatenate([vt_ref[0, j, hh * V_DIM:(hh + 1) * V_DIM, :], ones], axis=0)
            m_blk = jnp.max(s[hh], axis=0, keepdims=True)
            m_new = m_blk if j == 0 else jnp.maximum(m[hh], m_blk)
            part = _dot(vt, jnp.exp2(s[hh] - m_new).astype(BF16))
            acc[hh] = part if j == 0 else acc[hh] * jnp.exp2(m[hh] - m_new) + part
            m[hh] = m_new

    outs = [(acc[hh][0:V_DIM] / acc[hh][V_DIM:V_DIM + 1]).T for hh in heads]
    o_ref[0] = jnp.concatenate(outs, axis=1).astype(BF16)


def _attn(q, k_all, vt_all, tq):
    b, l_x, _ = q.shape
    s_total = k_all.shape[1]
    nb, kb = vt_all.shape[1], vt_all.shape[3]
    hw = ATTN_HEADS * HEAD_PAD
    return pl.pallas_call(
        _attn_kernel,
        out_shape=jax.ShapeDtypeStruct((b, l_x, MLA_HEADS * V_DIM), BF16),
        grid=(b, MLA_HEADS // ATTN_HEADS, l_x // tq),
        in_specs=[pl.BlockSpec((1, tq, hw), lambda bi, h, i: (bi, i, h)),
                  pl.BlockSpec((1, s_total, hw), lambda bi, h, i: (bi, 0, h)),
                  pl.BlockSpec((1, nb, ATTN_HEADS * V_DIM, kb), lambda bi, h, i: (bi, 0, h, 0))],
        out_specs=pl.BlockSpec((1, tq, ATTN_HEADS * V_DIM), lambda bi, h, i: (bi, i, h)),
        compiler_params=_params(3),
        name="attn",
    )(q, k_all, vt_all)


def _rope_swap(w):
    parts = []
    for base in (0, QK_ROPE // 2):
        u1 = w[..., base:base + QK_ROPE // 4]
        u2 = w[..., base + QK_ROPE // 4:base + QK_ROPE // 2]
        parts += [-u2, u1]
    return jnp.concatenate(parts, axis=-1)


def _rope_tables(l_x):
    half = QK_ROPE // 2
    inv = (ROPE_BASE ** (-np.arange(0, half, 2, dtype=np.float32) / half)).astype(np.float32)
    rows = l_x // GRID_W
    row = np.repeat(np.arange(rows), GRID_W).astype(np.float32)
    col = (np.arange(rows * GRID_W) % GRID_W).astype(np.float32)
    ar = (row[:, None] * inv).astype(np.float64)
    ac = (col[:, None] * inv).astype(np.float64)
    cos = np.concatenate([np.cos(ar), np.cos(ar), np.cos(ac), np.cos(ac)], axis=1).astype(np.float32)
    sin = np.concatenate([np.sin(ar), np.sin(ar), np.sin(ac), np.sin(ac)], axis=1).astype(np.float32)
    return cos, sin


def kernel(x, c, ctx, c_ctx, w_ada, b_ada, norm_ffn1, w_ffn1_in, w_ffn1_out, norm_mix, w_in, conv_w, conv_b, dt_bias, a_log, d_skip, ssd_norm, q_norm, w_uq, kv_norm, w_ukv, w_out, norm_ffn2, w_ffn2_in, w_ffn2_out, final_norm):
    b, l_x, d = x.shape
    l_c = ctx.shape[1]
    assert d == D_MODEL and w_ada.shape[0] == 1
    assert l_x % KEY_BLOCK == 0 and l_c % KEY_BLOCK == 0 and l_x % GRID_W == 0
    s_total = l_x + l_c

    c_rows = jnp.concatenate([c, c_ctx[None, :]], axis=0)
    pad = (-c_rows.shape[0]) % SUBLANES
    c_rows = jnp.pad(c_rows, ((0, pad), (0, 0)))
    m_all = _ada(c_rows, w_ada[0], b_ada)[:b + 1].reshape(b + 1, N_MOD, d)

    tm = _pick_tile(l_x, FFN_TILE)
    tmc = _pick_tile(b * l_c, FFN_TILE)
    x_tiles = l_x // tm
    m_lat = lambda i: i // x_tiles
    m_ctx = lambda i: b

    wi1, wo1 = w_ffn1_in[0].astype(BF16), w_ffn1_out[0].astype(BF16)
    x1 = _ffn(x.reshape(b * l_x, d), m_all, m_lat, norm_ffn1, wi1, wo1, 0, tm)
    c1 = _ffn(ctx.reshape(b * l_c, d), m_all, m_ctx, norm_ffn1, wi1, wo1, 0, tmc)

    w = w_in[0]
    w_kpe = w[:, SPLIT_CKV:]
    n_dt = N_DIR * SSD_HEADS
    small = jnp.concatenate([w[:, SPLIT_XBC:SPLIT_DT], jnp.zeros((d, KPE_LANE0 - QK_ROPE - n_dt), F32),
                             _rope_swap(w_kpe), w_kpe,
                             jnp.zeros((d, SMALL_W - KPE_LANE0 - QK_ROPE), F32)], axis=1)
    w_all = jnp.concatenate([w[:, :SPLIT_XBC], w[:, SPLIT_DT:SPLIT_CKV], small], axis=1).astype(BF16)
    hq = QK_NOPE + QK_ROPE
    wq = w_uq[0].reshape(Q_LORA, MLA_HEADS, hq)
    zq = jnp.zeros((Q_LORA, MLA_HEADS, HEAD_PAD - hq), F32)
    wqa = jnp.concatenate([wq, zq], axis=2).reshape(Q_LORA, MLA_PAD).astype(BF16)
    wkv = w_ukv[0].reshape(KV_LORA, MLA_HEADS, QK_NOPE + V_DIM)
    wuk = jnp.concatenate([wkv[:, :, :QK_NOPE], jnp.zeros((KV_LORA, MLA_HEADS, HEAD_PAD - QK_NOPE), F32)],
                          axis=2).reshape(KV_LORA, MLA_PAD).astype(BF16)
    wuv = wkv[:, :, QK_NOPE:].reshape(KV_LORA, MLA_HEADS * V_DIM).astype(BF16)
    cos, sin = _rope_tables(l_x)
    zpad = lambda n: np.zeros((l_x, n), np.float32)
    q_scale = np.float32(MLA_SCALE * math.log2(math.e))
    tqc = jnp.asarray(q_scale * np.concatenate([np.ones((l_x, QK_NOPE), np.float32), cos, zpad(HEAD_PAD - hq)], axis=1))
    sign = np.tile(np.repeat(np.float32([-1.0, 1.0]), QK_ROPE // 4), 2)
    tqs = jnp.asarray(q_scale * np.concatenate([zpad(QK_NOPE), sin * sign, zpad(HEAD_PAD - hq)], axis=1))
    tk_lat = jnp.asarray(np.concatenate([zpad(KPE_LANE0 - QK_ROPE), sin, cos,
                                         zpad(SMALL_W - KPE_LANE0 - QK_ROPE)], axis=1))
    tk_ctx_np = np.zeros((l_c, SMALL_W), np.float32)
    tk_ctx_np[:, KPE_LANE0:KPE_LANE0 + QK_ROPE] = 1.0
    tk_ctx = jnp.asarray(tk_ctx_np)

    cw = jnp.pad(conv_w[0], ((0, SUBLANES - SSD_CONV), (0, 0)))
    shared = (norm_mix, w_all, cw, conv_b, kv_norm, wuk, wuv)
    u_all, small_all, k_all, v_all, z, q = _in_proj(
        x1.reshape(b, l_x, d), m_all, lambda bi: bi, *shared, tk_lat, _pick_tile(l_x, PROJ_TILE), 0, s_total,
        q_args=(q_norm, wqa, tqc, tqs))
    u_all, small_all, k_all, v_all = _in_proj(
        c1.reshape(b, l_c, d), m_all, lambda bi: b, *shared, tk_ctx,
        _pick_tile(math.gcd(l_c, l_x), PROJ_CTX_TILE), l_x, s_total,
        alias_bufs=(u_all, small_all, k_all, v_all))

    lane_pad = lambda v: jnp.pad(v.reshape(1, -1), ((0, 0), (DT_LANE0, SMALL_W - DT_LANE0 - v.size)))
    dtb = lane_pad(dt_bias[0])
    aneg = lane_pad(-jnp.exp(a_log[0].astype(F32)))
    dskip = jnp.repeat(d_skip[0], SSD_HEAD_DIM)[None, :]
    yf, yb = _ssd(u_all, small_all, dtb, aneg, dskip, l_x, l_c)

    mla = _attn(q, k_all, v_all, _pick_tile(l_x, ATTN_Q_TILE))

    wo = w_out[0]
    ws = wo[:SSD_WIDTH].astype(BF16)
    wm = wo[SSD_WIDTH:].astype(BF16)
    n = b * l_x
    mix = (yf.reshape(n, SSD_WIDTH), yb.reshape(n, SSD_WIDTH), z.reshape(n, SSD_WIDTH),
           mla.reshape(n, MLA_HEADS * V_DIM), ssd_norm, ws, wm, final_norm[None, :])
    out = _ffn(x1, m_all, m_lat, norm_ffn2, w_ffn2_in[0].astype(BF16), w_ffn2_out[0].astype(BF16), 6, tm,
               mix=mix)
    return out.reshape(b, l_x, d)
```

```python
import functools
import math

import jax
import jax.numpy as jnp
import numpy as np
from jax import lax
from jax.experimental import pallas as pl
from jax.experimental.pallas import tpu as pltpu

F32 = jnp.float32
BF16 = jnp.bfloat16

D_MODEL = 1024
GRID_W = 64
N_MOD = 9
D_FF = 2816
EPS = 1e-6
SSD_HEADS = 8
SSD_HEAD_DIM = 64
SSD_WIDTH = SSD_HEADS * SSD_HEAD_DIM
SSD_GROUPS = 2
SSD_STATE = 128
SSD_CONV = 5
SSD_CHUNK = 128
N_DIR = 2
CONV_CH = SSD_WIDTH + 2 * SSD_GROUPS * SSD_STATE
MLA_HEADS = 8
QK_NOPE = 64
QK_ROPE = 32
V_DIM = 64
Q_LORA = 768
KV_LORA = 256
MLA_SCALE = (QK_NOPE + QK_ROPE) ** -0.5
ROPE_BASE = 10000.0
SPLIT_XBC = SSD_WIDTH + CONV_CH
SPLIT_DT = SPLIT_XBC + N_DIR * SSD_HEADS
SPLIT_CQ = SPLIT_DT + Q_LORA
SPLIT_CKV = SPLIT_CQ + KV_LORA

LANES = 128
SUBLANES = 8
VMEM_LIMIT = 56 * 1024 * 1024

HEAD_PAD = LANES
MLA_PAD = MLA_HEADS * HEAD_PAD
SMALL_W = LANES
DT_LANE0 = 0
KPE_LANE0 = QK_NOPE
KEY_BLOCK = 256

ADA_COLS = 1024
FFN_TILE = 512
FF_CHUNK = 256
PROJ_TILE = 1024
PROJ_CTX_TILE = 256
SSD_BLOCK_CHUNKS = 2
ATTN_Q_TILE = 1024
ATTN_HEADS = LANES // V_DIM
ATTN_LOOKAHEAD = 3


def _dot(a, b):
    return jnp.dot(a, b, preferred_element_type=F32)


def _dot_nt(a, b):
    return lax.dot_general(a, b, (((1,), (1,)), ((), ())), preferred_element_type=F32)


def _dot_tn(a, b):
    return lax.dot_general(a, b, (((0,), (0,)), ((), ())), preferred_element_type=F32)


def _silu(x):
    return x * (1.0 / (1.0 + jnp.exp(-x)))


def _rms(x, g):
    return x * lax.rsqrt(jnp.mean(x * x, axis=-1, keepdims=True) + EPS) * g


def _const_spec(shape):
    zeros = (0,) * len(shape)
    return pl.BlockSpec(shape, lambda *_: zeros, pipeline_mode=pl.Buffered(1))


def _params(n_axes):
    return pltpu.CompilerParams(dimension_semantics=("arbitrary",) * n_axes,
                                vmem_limit_bytes=VMEM_LIMIT)


def _pick_tile(n, pref):
    t = min(pref, n)
    while n % t:
        t //= 2
    return t


def _ada_kernel(c_ref, w_ref, b_ref, o_ref):
    s = _silu(c_ref[...]).astype(BF16)
    o_ref[...] = _dot(s, w_ref[...].astype(BF16)) + b_ref[...]


def _ada(c_rows, w, b):
    rows, d = c_rows.shape
    n = w.shape[1]
    tn = _pick_tile(n, ADA_COLS)
    return pl.pallas_call(
        _ada_kernel,
        out_shape=jax.ShapeDtypeStruct((rows, n), F32),
        grid=(n // tn,),
        in_specs=[pl.BlockSpec((rows, d), lambda j: (0, 0)),
                  pl.BlockSpec((d, tn), lambda j: (0, j)),
                  pl.BlockSpec((1, tn), lambda j: (0, j))],
        out_specs=pl.BlockSpec((rows, tn), lambda j: (0, j)),
        compiler_params=_params(1),
        name="ada",
    )(c_rows, w, b)


def _ffn_kernel(x_ref, m_ref, g_ref, wi_ref, wo_ref, *rest, k, final):
    x = x_ref[...]
    m = m_ref[0]
    if final:
        yf_ref, yb_ref, z_ref, a_ref, sn_ref, ws_ref, wm_ref, fg_ref, o_ref, acc_ref = rest
        y = (yf_ref[...] + yb_ref[...]) * _silu(z_ref[...])
        ssd = _rms(y, sn_ref[...]).astype(BF16)
        x = x + m[k - 1:k] * (_dot(ssd, ws_ref[...]) + _dot(a_ref[...], wm_ref[...]))
    else:
        o_ref, acc_ref = rest
    shift, scale, gate = m[k:k + 1], m[k + 1:k + 2], m[k + 2:k + 3]
    hn = (_rms(x, g_ref[...]) * (1 + scale) + shift).astype(BF16)
    for j in range(D_FF // FF_CHUNK):
        lo = j * FF_CHUNK
        g = _dot(hn, wi_ref[:, lo:lo + FF_CHUNK])
        u = _dot(hn, wi_ref[:, D_FF + lo:D_FF + lo + FF_CHUNK])
        a = (_silu(g) * u).astype(BF16)
        part = _dot(a, wo_ref[lo:lo + FF_CHUNK, :])
        if j == 0:
            acc_ref[...] = part
        else:
            acc_ref[...] += part
    y = x + 0.5 * gate * acc_ref[...]
    if final:
        y = _rms(y, fg_ref[...])
    o_ref[...] = y


def _ffn(x2d, m_all, m_index, g, wi, wo, k, tm, mix=None):
    n, d = x2d.shape
    final = mix is not None
    tok = lambda c: pl.BlockSpec((tm, c), lambda i: (i, 0))
    in_specs = [tok(d),
                pl.BlockSpec((1, N_MOD, d), lambda i: (m_index(i), 0, 0)),
                _const_spec((1, d)),
                _const_spec(wi.shape),
                _const_spec(wo.shape)]
    args = [x2d, m_all, g, wi, wo]
    if final:
        in_specs += [tok(a.shape[1]) for a in mix[:4]] + [_const_spec(a.shape) for a in mix[4:]]
        args += list(mix)
    return pl.pallas_call(
        functools.partial(_ffn_kernel, k=k, final=final),
        out_shape=jax.ShapeDtypeStruct((n, d), F32),
        grid=(n // tm,),
        in_specs=in_specs,
        out_specs=pl.BlockSpec((tm, d), lambda i: (i, 0)),
        scratch_shapes=[pltpu.VMEM((tm, d), F32)],
        compiler_params=_params(1),
        name="ffn_final" if final else "ffn",
    )(*args)


def _in_proj_kernel(x_ref, xp_ref, xn_ref, m_ref, g_ref, w_ref, cw_ref, cb_ref, kvn_ref, wuk_ref,
                    wuv_ref, tk_ref, *rest, with_q, n_alias):
    if with_q:
        qn_ref, wqa_ref, tqc_ref, tqs_ref = rest[:4]
        rest = rest[4:]
    rest = rest[n_alias:]
    u_ref, small_ref, k_ref, v_ref = rest[:4]
    if with_q:
        z_ref, q_ref = rest[4:6]
    ext_ref = rest[-1]
    m = m_ref[0]
    tm = x_ref.shape[1]
    i = pl.program_id(1)
    x_ext = jnp.concatenate([x_ref[0], xp_ref[0], xn_ref[0]], axis=0)
    hn_ext = (_rms(x_ext, g_ref[...]) * (1 + m[4:5]) + m[3:4]).astype(BF16)
    hn = hn_ext[0:tm]
    o_xbc = SSD_WIDTH
    o_cq = o_xbc + CONV_CH
    o_ckv = o_cq + Q_LORA
    o_small = o_ckv + KV_LORA

    xbc = _dot(hn_ext, w_ref[:, o_xbc:o_cq])
    if with_q:
        z_ref[0] = _dot(hn, w_ref[:, 0:o_xbc])
        cq_raw = _dot(hn, w_ref[:, o_cq:o_ckv])
    ckv_raw = _dot(hn, w_ref[:, o_ckv:o_small])
    small = _dot(hn, w_ref[:, o_small:o_small + SMALL_W])
    small_ref[0] = small

    has_prev = (i > 0).astype(F32)
    has_next = (i < pl.num_programs(1) - 1).astype(F32)
    ext_ref[0:SUBLANES] = xbc[tm:tm + SUBLANES] * has_prev
    ext_ref[SUBLANES:SUBLANES + tm] = xbc[0:tm]
    ext_ref[SUBLANES + tm:] = xbc[tm + SUBLANES:] * has_next
    conv = cb_ref[...]
    for tap in range(SSD_CONV):
        conv = conv + ext_ref[pl.ds(SUBLANES - SSD_CONV // 2 + tap, tm), :] * cw_ref[tap:tap + 1, :]
    u_ref[0] = _silu(conv).astype(BF16)

    ckv = _rms(ckv_raw, kvn_ref[...]).astype(BF16)
    t = small * tk_ref[...]
    lane = lax.broadcasted_iota(jnp.int32, (tm, LANES), 1)
    on_rope = jnp.logical_and(lane >= KPE_LANE0, lane < KPE_LANE0 + QK_ROPE)
    kpe = jnp.where(on_rope, t + pltpu.roll(t, QK_ROPE, axis=1), 0.0)
    k_nope = _dot(ckv, wuk_ref[...])
    for h in range(MLA_HEADS):
        hs = slice(h * HEAD_PAD, (h + 1) * HEAD_PAD)
        k_ref[0, :, hs] = (k_nope[:, hs] + kpe).astype(BF16)
    vt = _dot(ckv, wuv_ref[...]).T.astype(BF16)
    for n in range(tm // KEY_BLOCK):
        v_ref[0, n] = vt[:, n * KEY_BLOCK:(n + 1) * KEY_BLOCK]
    if with_q:
        cq = _rms(cq_raw, qn_ref[...]).astype(BF16)
        qa = _dot(cq, wqa_ref[...])
        tqc, tqs = tqc_ref[...], tqs_ref[...]
        first_half = lax.broadcasted_iota(jnp.int32, (tm, HEAD_PAD), 1) % (QK_ROPE // 2) < QK_ROPE // 4
        for h in range(MLA_HEADS):
            hs = slice(h * HEAD_PAD, (h + 1) * HEAD_PAD)
            qh = qa[:, hs]
            partner = jnp.where(first_half, pltpu.roll(qh, HEAD_PAD - QK_ROPE // 4, axis=1),
                                pltpu.roll(qh, QK_ROPE // 4, axis=1))
            q_ref[0, :, hs] = (qh * tqc + partner * tqs).astype(BF16)


def _in_proj(h3d, m_all, m_index, g, w_all, cw, cb, kvn, wuk, wuv, tk, tm, row0, s_total,
             q_args=None, alias_bufs=()):
    b, rows, d = h3d.shape
    assert tm % KEY_BLOCK == 0 and row0 % tm == 0
    with_q = q_args is not None
    blk0 = row0 // tm
    hb, n8 = tm // SUBLANES, rows // SUBLANES
    tok = lambda c: pl.BlockSpec((1, tm, c), lambda bi, i: (bi, i, 0))
    comb = lambda c: pl.BlockSpec((1, tm, c), lambda bi, i: (bi, blk0 + i, 0))
    table = pl.BlockSpec((tm, LANES), lambda bi, i: (i, 0))
    in_specs = [tok(d),
                pl.BlockSpec((1, SUBLANES, d), lambda bi, i: (bi, jnp.maximum(i * hb - 1, 0), 0)),
                pl.BlockSpec((1, SUBLANES, d), lambda bi, i: (bi, jnp.minimum(i * hb + hb, n8 - 1), 0)),
                pl.BlockSpec((1, N_MOD, d), lambda bi, i: (m_index(bi), 0, 0)),
                _const_spec(g.shape), _const_spec(w_all.shape), _const_spec(cw.shape), _const_spec(cb.shape),
                _const_spec(kvn.shape), _const_spec(wuk.shape), _const_spec(wuv.shape), table]
    args = [h3d, h3d, h3d, m_all, g, w_all, cw, cb, kvn, wuk, wuv, tk]
    out_shape = [jax.ShapeDtypeStruct((b, s_total, CONV_CH), BF16),
                 jax.ShapeDtypeStruct((b, s_total, SMALL_W), F32),
                 jax.ShapeDtypeStruct((b, s_total, MLA_PAD), BF16),
                 jax.ShapeDtypeStruct((b, s_total // KEY_BLOCK, MLA_HEADS * V_DIM, KEY_BLOCK), BF16)]
    out_specs = [comb(CONV_CH), comb(SMALL_W), comb(MLA_PAD),
                 pl.BlockSpec((1, tm // KEY_BLOCK, MLA_HEADS * V_DIM, KEY_BLOCK),
                              lambda bi, i: (bi, blk0 + i, 0, 0))]
    if with_q:
        qn, wqa, tqc, tqs = q_args
        in_specs += [_const_spec(qn.shape), _const_spec(wqa.shape), table, table]
        args += [qn, wqa, tqc, tqs]
        out_shape += [jax.ShapeDtypeStruct((b, rows, SSD_WIDTH), F32),
                      jax.ShapeDtypeStruct((b, rows, MLA_PAD), BF16)]
        out_specs += [tok(SSD_WIDTH), tok(MLA_PAD)]
    aliases = {len(args) + j: j for j in range(len(alias_bufs))}
    in_specs += [pl.BlockSpec(memory_space=pl.ANY)] * len(alias_bufs)
    args += list(alias_bufs)
    return pl.pallas_call(
        functools.partial(_in_proj_kernel, with_q=with_q, n_alias=len(alias_bufs)),
        out_shape=out_shape,
        grid=(b, rows // tm),
        in_specs=in_specs,
        out_specs=out_specs,
        input_output_aliases=aliases,
        scratch_shapes=[pltpu.VMEM((tm + 2 * SUBLANES, CONV_CH), F32)],
        compiler_params=_params(2),
        name="in_proj_q" if with_q else "in_proj_ctx",
    )(*args)


def _ssd_kernel(uf_ref, sf_ref, ub_ref, sb_ref, dtb_ref, aneg_ref, dskip_ref, ex_ref, yf_ref, yb_ref,
                hf_ref, hb_ref, *, ncb):
    q = SSD_CHUNK
    t = pl.program_id(1)
    n_sub = uf_ref.shape[1] // q
    u_refs, small_refs, h_refs, y_refs = (uf_ref, ub_ref), (sf_ref, sb_ref), (hf_ref, hb_ref), (yf_ref, yb_ref)
    lanes = [(d, k if d == 0 else n_sub - 1 - k) for k in range(n_sub) for d in (0, 1)]
    rows = [slice(sub * q, (sub + 1) * q) for _, sub in lanes]
    ids = range(len(lanes))

    @pl.when(t == 0)
    def _():
        hf_ref[...] = jnp.zeros_like(hf_ref)
        hb_ref[...] = jnp.zeros_like(hb_ref)

    o_b = SSD_WIDTH
    o_c = SSD_WIDTH + SSD_GROUPS * SSD_STATE
    xs = [u_refs[d][0, rows[i], 0:o_b].astype(F32) for i, (d, _) in enumerate(lanes)]
    bm = [u_refs[d][0, rows[i], o_b:o_c] for i, (d, _) in enumerate(lanes)]
    cm = [u_refs[d][0, rows[i], o_c:CONV_CH] for i, (d, _) in enumerate(lanes)]
    gslice = [slice(g * SSD_STATE, (g + 1) * SSD_STATE) for g in range(SSD_GROUPS)]
    cbm = [[_dot_nt(cm[i][:, gs], bm[i][:, gs]) for i in ids] for gs in gslice]

    ri = lax.broadcasted_iota(jnp.int32, (q, q), 0)
    ci = lax.broadcasted_iota(jnp.int32, (q, q), 1)
    keep = [ri >= ci, ri <= ci]
    last = [q - 1, 0]
    tri = [jnp.concatenate([jnp.where(keep[d], 1.0, 0.0).astype(BF16)] * 3, axis=1) for d in (0, 1)]
    dtv, parts = [], []
    for i, (d, _) in enumerate(lanes):
        pre = small_refs[d][0, rows[i], :] + dtb_ref[...]
        dtv.append(jnp.maximum(pre, 0.0) + jnp.log(1.0 + jnp.exp(-jnp.abs(pre))))
        a = dtv[i] * aneg_ref[...]
        a1 = a.astype(BF16)
        r1 = a - a1.astype(F32)
        a2 = r1.astype(BF16)
        a3 = (r1 - a2.astype(F32)).astype(BF16)
        parts.append(jnp.concatenate([a1, a2, a3], axis=0))
    acum = [_dot(tri[lanes[i][0]], parts[i]) for i in ids]
    acum_t = [acum[i].T for i in ids]
    dend = [jnp.exp(acum[i][last[d]:last[d] + 1, :] - acum[i]) for i, (d, _) in enumerate(lanes)]
    dstart = [jnp.exp(acum[i]) for i in ids]

    head_lane = lambda d, h: DT_LANE0 + SSD_HEADS * d + h

    def expand(arr, d):
        hi = arr.astype(BF16)
        lo = (arr - hi.astype(F32)).astype(BF16)
        return _dot(jnp.concatenate([hi, lo], axis=1), ex_ref[d])

    dstart_x = [expand(dstart[i], d) for i, (d, _) in enumerate(lanes)]
    xdt = [xs[i] * expand(dtv[i], d) for i, (d, _) in enumerate(lanes)]
    xend = [(xdt[i] * expand(dend[i], d)).astype(BF16) for i, (d, _) in enumerate(lanes)]
    gw = SSD_WIDTH // SSD_GROUPS
    hpg = SSD_HEADS // SSD_GROUPS
    glane = lax.broadcasted_iota(jnp.int32, (q, gw), 1)
    ys = [[] for _ in ids]
    for g in range(SSD_GROUPS):
        gs = gslice[g]
        gc = slice(g * gw, (g + 1) * gw)
        yg = [None for _ in ids]
        for h0 in range(0, hpg, 2):
            for i, (d, _) in enumerate(lanes):
                lmats, xmasks = [], []
                for hh in (h0, h0 + 1):
                    hl = head_lane(d, g * hpg + hh)
                    seg = acum[i][:, hl:hl + 1] - acum_t[i][hl:hl + 1, :]
                    lmats.append((cbm[g][i] * jnp.exp(jnp.where(keep[d], seg, -jnp.inf))).astype(BF16))
                    in_head = jnp.logical_and(glane >= hh * SSD_HEAD_DIM, glane < (hh + 1) * SSD_HEAD_DIM)
                    xmasks.append(jnp.where(in_head, xdt[i][:, gc], 0.0).astype(BF16))
                part = _dot(jnp.concatenate(lmats, axis=1), jnp.concatenate(xmasks, axis=0))
                yg[i] = part if yg[i] is None else yg[i] + part
        for i, (d, _) in enumerate(lanes):
            hst = h_refs[d][g]
            yoff = _dot(cm[i][:, gs], hst.astype(BF16)) * dstart_x[i][:, gc]
            ys[i].append(yg[i] + yoff)
            h_refs[d][g] = (dstart_x[i][last[d]:last[d] + 1, gc] * hst
                            + _dot_tn(bm[i][:, gs], xend[i][:, gc]))

    @pl.when(t >= ncb)
    def _():
        for i, (d, _) in enumerate(lanes):
            y = jnp.concatenate(ys[i], axis=1)
            if d == 0:
                y = y + dskip_ref[...] * xs[i]
            y_refs[d][0, rows[i], :] = y


def _ssd(u_all, small_all, dtb, aneg, dskip, l_x, l_c):
    b = u_all.shape[0]
    n_sub = SSD_BLOCK_CHUNKS if (l_x % (SSD_BLOCK_CHUNKS * SSD_CHUNK) == 0
                                 and l_c % (SSD_BLOCK_CHUNKS * SSD_CHUNK) == 0) else 1
    rb = n_sub * SSD_CHUNK
    nxb, ncb = l_x // rb, l_c // rb
    fwd = lambda t: jnp.where(t < ncb, nxb + t, t - ncb)
    bwd = lambda t: nxb + ncb - 1 - t
    block_spec = lambda w, order: pl.BlockSpec((1, rb, w), lambda bi, t: (bi, order(t), 0))
    y_spec = lambda order: pl.BlockSpec((1, rb, SSD_WIDTH), lambda bi, t: (bi, order(jnp.maximum(t, ncb)), 0))
    y_shape = jax.ShapeDtypeStruct((b, l_x, SSD_WIDTH), F32)
    h_buf = pltpu.VMEM((SSD_GROUPS, SSD_STATE, SSD_WIDTH // SSD_GROUPS), F32)
    ex_np = np.zeros((N_DIR, SMALL_W, SSD_WIDTH), np.float32)
    for dd in range(N_DIR):
        for h in range(SSD_HEADS):
            ex_np[dd, DT_LANE0 + SSD_HEADS * dd + h, h * SSD_HEAD_DIM:(h + 1) * SSD_HEAD_DIM] = 1.0
    ex = jnp.asarray(np.concatenate([ex_np, ex_np], axis=1), BF16)
    return pl.pallas_call(
        functools.partial(_ssd_kernel, ncb=ncb),
        out_shape=[y_shape, y_shape],
        grid=(b, nxb + ncb),
        in_specs=[block_spec(CONV_CH, fwd), block_spec(SMALL_W, fwd),
                  block_spec(CONV_CH, bwd), block_spec(SMALL_W, bwd),
                  _const_spec(dtb.shape), _const_spec(aneg.shape), _const_spec(dskip.shape),
                  _const_spec(ex.shape)],
        out_specs=[y_spec(fwd), y_spec(bwd)],
        scratch_shapes=[h_buf, h_buf],
        compiler_params=_params(2),
        name="ssd",
    )(u_all, small_all, u_all, small_all, dtb, aneg, dskip, ex)


def _attn_kernel(q_ref, k_ref, vt_ref, o_ref):
    nb = vt_ref.shape[1]
    kb = vt_ref.shape[3]
    heads = range(ATTN_HEADS)
    ones = jnp.ones((2 * SUBLANES, kb), BF16)
    q = [q_ref[0, :, hh * HEAD_PAD:(hh + 1) * HEAD_PAD] for hh in heads]

    def scores(j):
        return [_dot_nt(k_ref[0, j * kb:(j + 1) * kb, hh * HEAD_PAD:(hh + 1) * HEAD_PAD], q[hh])
                for hh in heads]

    m = [None] * ATTN_HEADS
    acc = [None] * ATTN_HEADS
    pending = [scores(j) for j in range(min(ATTN_LOOKAHEAD, nb))]
    for j in range(nb):
        s = pending.pop(0)
        if j + ATTN_LOOKAHEAD < nb:
            pending.append(scores(j + ATTN_LOOKAHEAD))
        for hh in heads:
            vt = jnp.concatenate([vt_ref[0, j, hh * V_DIM:(hh + 1) * V_DIM, :], ones], axis=0)
            m_blk = jnp.max(s[hh], axis=0, keepdims=True)
            m_new = m_blk if j == 0 else jnp.maximum(m[hh], m_blk)
            part = _dot(vt, jnp.exp2(s[hh] - m_new).astype(BF16))
            acc[hh] = part if j == 0 else acc[hh] * jnp.exp2(m[hh] - m_new) + part
            m[hh] = m_new

    outs = [(acc[hh][0:V_DIM] / acc[hh][V_DIM:V_DIM + 1]).T for hh in heads]
    o_ref[0] = jnp.concatenate(outs, axis=1).astype(BF16)


def _attn(q, k_all, vt_all, tq):
    b, l_x, _ = q.shape
    s_total = k_all.shape[1]
    nb, kb = vt_all.shape[1], vt_all.shape[3]
    hw = ATTN_HEADS * HEAD_PAD
    return pl.pallas_call(
        _attn_kernel,
        out_shape=jax.ShapeDtypeStruct((b, l_x, MLA_HEADS * V_DIM), BF16),
        grid=(b, MLA_HEADS // ATTN_HEADS, l_x // tq),
        in_specs=[pl.BlockSpec((1, tq, hw), lambda bi, h, i: (bi, i, h)),
                  pl.BlockSpec((1, s_total, hw), lambda bi, h, i: (bi, 0, h)),
                  pl.BlockSpec((1, nb, ATTN_HEADS * V_DIM, kb), lambda bi, h, i: (bi, 0, h, 0))],
        out_specs=pl.BlockSpec((1, tq, ATTN_HEADS * V_DIM), lambda bi, h, i: (bi, i, h)),
        compiler_params=_params(3),
        name="attn",
    )(q, k_all, vt_all)


def _rope_swap(w):
    parts = []
    for base in (0, QK_ROPE // 2):
        u1 = w[..., base:base + QK_ROPE // 4]
        u2 = w[..., base + QK_ROPE // 4:base + QK_ROPE // 2]
        parts += [-u2, u1]
    return jnp.concatenate(parts, axis=-1)


def _rope_tables(l_x):
    half = QK_ROPE // 2
    inv = (ROPE_BASE ** (-np.arange(0, half, 2, dtype=np.float32) / half)).astype(np.float32)
    rows = l_x // GRID_W
    row = np.repeat(np.arange(rows), GRID_W).astype(np.float32)
    col = (np.arange(rows * GRID_W) % GRID_W).astype(np.float32)
    ar = (row[:, None] * inv).astype(np.float64)
    ac = (col[:, None] * inv).astype(np.float64)
    cos = np.concatenate([np.cos(ar), np.cos(ar), np.cos(ac), np.cos(ac)], axis=1).astype(np.float32)
    sin = np.concatenate([np.sin(ar), np.sin(ar), np.sin(ac), np.sin(ac)], axis=1).astype(np.float32)
    return cos, sin


def kernel(x, c, ctx, c_ctx, w_ada, b_ada, norm_ffn1, w_ffn1_in, w_ffn1_out, norm_mix, w_in, conv_w, conv_b, dt_bias, a_log, d_skip, ssd_norm, q_norm, w_uq, kv_norm, w_ukv, w_out, norm_ffn2, w_ffn2_in, w_ffn2_out, final_norm):
    b, l_x, d = x.shape
    l_c = ctx.shape[1]
    assert d == D_MODEL and w_ada.shape[0] == 1
    assert l_x % KEY_BLOCK == 0 and l_c % KEY_BLOCK == 0 and l_x % GRID_W == 0
    s_total = l_x + l_c

    c_rows = jnp.concatenate([c, c_ctx[None, :]], axis=0)
    pad = (-c_rows.shape[0]) % SUBLANES
    c_rows = jnp.pad(c_rows, ((0, pad), (0, 0)))
    m_all = _ada(c_rows, w_ada[0], b_ada)[:b + 1].reshape(b + 1, N_MOD, d)

    tm = _pick_tile(l_x, FFN_TILE)
    tmc = _pick_tile(b * l_c, FFN_TILE)
    x_tiles = l_x // tm
    m_lat = lambda i: i // x_tiles
    m_ctx = lambda i: b

    wi1, wo1 = w_ffn1_in[0].astype(BF16), w_ffn1_out[0].astype(BF16)
    x1 = _ffn(x.reshape(b * l_x, d), m_all, m_lat, norm_ffn1, wi1, wo1, 0, tm)
    c1 = _ffn(ctx.reshape(b * l_c, d), m_all, m_ctx, norm_ffn1, wi1, wo1, 0, tmc)

    w = w_in[0]
    w_kpe = w[:, SPLIT_CKV:]
    n_dt = N_DIR * SSD_HEADS
    small = jnp.concatenate([w[:, SPLIT_XBC:SPLIT_DT], jnp.zeros((d, KPE_LANE0 - QK_ROPE - n_dt), F32),
                             _rope_swap(w_kpe), w_kpe,
                             jnp.zeros((d, SMALL_W - KPE_LANE0 - QK_ROPE), F32)], axis=1)
    w_all = jnp.concatenate([w[:, :SPLIT_XBC], w[:, SPLIT_DT:SPLIT_CKV], small], axis=1).astype(BF16)
    hq = QK_NOPE + QK_ROPE
    wq = w_uq[0].reshape(Q_LORA, MLA_HEADS, hq)
    zq = jnp.zeros((Q_LORA, MLA_HEADS, HEAD_PAD - hq), F32)
    wqa = jnp.concatenate([wq, zq], axis=2).reshape(Q_LORA, MLA_PAD).astype(BF16)
    wkv = w_ukv[0].reshape(KV_LORA, MLA_HEADS, QK_NOPE + V_DIM)
    wuk = jnp.concatenate([wkv[:, :, :QK_NOPE], jnp.zeros((KV_LORA, MLA_HEADS, HEAD_PAD - QK_NOPE), F32)],
                          axis=2).reshape(KV_LORA, MLA_PAD).astype(BF16)
    wuv = wkv[:, :, QK_NOPE:].reshape(KV_LORA, MLA_HEADS * V_DIM).astype(BF16)
    cos, sin = _rope_tables(l_x)
    zpad = lambda n: np.zeros((l_x, n), np.float32)
    q_scale = np.float32(MLA_SCALE * math.log2(math.e))
    tqc = jnp.asarray(q_scale * np.concatenate([np.ones((l_x, QK_NOPE), np.float32), cos, zpad(HEAD_PAD - hq)], axis=1))
    sign = np.tile(np.repeat(np.float32([-1.0, 1.0]), QK_ROPE // 4), 2)
    tqs = jnp.asarray(q_scale * np.concatenate([zpad(QK_NOPE), sin * sign, zpad(HEAD_PAD - hq)], axis=1))
    tk_lat = jnp.asarray(np.concatenate([zpad(KPE_LANE0 - QK_ROPE), sin, cos,
                                         zpad(SMALL_W - KPE_LANE0 - QK_ROPE)], axis=1))
    tk_ctx_np = np.zeros((l_c, SMALL_W), np.float32)
    tk_ctx_np[:, KPE_LANE0:KPE_LANE0 + QK_ROPE] = 1.0
    tk_ctx = jnp.asarray(tk_ctx_np)

    cw = jnp.pad(conv_w[0], ((0, SUBLANES - SSD_CONV), (0, 0)))
    shared = (norm_mix, w_all, cw, conv_b, kv_norm, wuk, wuv)
    u_all, small_all, k_all, v_all, z, q = _in_proj(
        x1.reshape(b, l_x, d), m_all, lambda bi: bi, *shared, tk_lat, _pick_tile(l_x, PROJ_TILE), 0, s_total,
        q_args=(q_norm, wqa, tqc, tqs))
    u_all, small_all, k_all, v_all = _in_proj(
        c1.reshape(b, l_c, d), m_all, lambda bi: b, *shared, tk_ctx,
        _pick_tile(math.gcd(l_c, l_x), PROJ_CTX_TILE), l_x, s_total,
        alias_bufs=(u_all, small_all, k_all, v_all))

    lane_pad = lambda v: jnp.pad(v.reshape(1, -1), ((0, 0), (DT_LANE0, SMALL_W - DT_LANE0 - v.size)))
    dtb = lane_pad(dt_bias[0])
    aneg = lane_pad(-jnp.exp(a_log[0].astype(F32)))
    dskip = jnp.repeat(d_skip[0], SSD_HEAD_DIM)[None, :]
    yf, yb = _ssd(u_all, small_all, dtb, aneg, dskip, l_x, l_c)

    mla = _attn(q, k_all, v_all, _pick_tile(l_x, ATTN_Q_TILE))

    wo = w_out[0]
    ws = wo[:SSD_WIDTH].astype(BF16)
    wm = wo[SSD_WIDTH:].astype(BF16)
    n = b * l_x
    mix = (yf.reshape(n, SSD_WIDTH), yb.reshape(n, SSD_WIDTH), z.reshape(n, SSD_WIDTH),
           mla.reshape(n, MLA_HEADS * V_DIM), ssd_norm, ws, wm, final_norm[None, :])
    out = _ffn(x1, m_all, m_lat, norm_ffn2, w_ffn2_in[0].astype(BF16), w_ffn2_out[0].astype(BF16), 6, tm,
               mix=mix)
    return out.reshape(b, l_x, d)
```

```python
import functools
import math

import jax
import jax.numpy as jnp
import numpy as np
from jax import lax
from jax.experimental import pallas as pl
from jax.experimental.pallas import tpu as pltpu

F32 = jnp.float32
BF16 = jnp.bfloat16

D_MODEL = 1024
GRID_W = 64
N_MOD = 9
D_FF = 2816
EPS = 1e-6
SSD_HEADS = 8
SSD_HEAD_DIM = 64
SSD_WIDTH = SSD_HEADS * SSD_HEAD_DIM
SSD_GROUPS = 2
SSD_STATE = 128
SSD_CONV = 5
SSD_CHUNK = 128
N_DIR = 2
CONV_CH = SSD_WIDTH + 2 * SSD_GROUPS * SSD_STATE
MLA_HEADS = 8
QK_NOPE = 64
QK_ROPE = 32
V_DIM = 64
Q_LORA = 768
KV_LORA = 256
MLA_SCALE = (QK_NOPE + QK_ROPE) ** -0.5
ROPE_BASE = 10000.0
SPLIT_XBC = SSD_WIDTH + CONV_CH
SPLIT_DT = SPLIT_XBC + N_DIR * SSD_HEADS
SPLIT_CQ = SPLIT_DT + Q_LORA
SPLIT_CKV = SPLIT_CQ + KV_LORA

LANES = 128
SUBLANES = 8
VMEM_LIMIT = 56 * 1024 * 1024

HEAD_PAD = LANES
MLA_PAD = MLA_HEADS * HEAD_PAD
SMALL_W = LANES
DT_LANE0 = 0
KPE_LANE0 = QK_NOPE
KEY_BLOCK = 256

ADA_COLS = 1024
FFN_TILE = 512
FFN1_TILE = 1024
FF_CHUNK = 256
PROJ_TILE = 1024
PROJ_CTX_TILE = 256
SSD_BLOCK_CHUNKS = 2
ATTN_Q_TILE = 1024
ATTN_HEADS = LANES // V_DIM
ATTN_LOOKAHEAD = 2


def _dot(a, b):
    return jnp.dot(a, b, preferred_element_type=F32)


def _dot_nt(a, b):
    return lax.dot_general(a, b, (((1,), (1,)), ((), ())), preferred_element_type=F32)


def _dot_tn(a, b):
    return lax.dot_general(a, b, (((0,), (0,)), ((), ())), preferred_element_type=F32)


def _silu(x):
    return x * (1.0 / (1.0 + jnp.exp(-x)))


def _rms(x, g):
    return x * lax.rsqrt(jnp.mean(x * x, axis=-1, keepdims=True) + EPS) * g


def _const_spec(shape):
    zeros = (0,) * len(shape)
    return pl.BlockSpec(shape, lambda *_: zeros, pipeline_mode=pl.Buffered(1))


def _params(n_axes):
    return pltpu.CompilerParams(dimension_semantics=("arbitrary",) * n_axes,
                                vmem_limit_bytes=VMEM_LIMIT)


def _pick_tile(n, pref):
    t = min(pref, n)
    while n % t:
        t //= 2
    return t


def _ada_kernel(c_ref, w_ref, b_ref, o_ref):
    s = _silu(c_ref[...]).astype(BF16)
    o_ref[...] = _dot(s, w_ref[...].astype(BF16)) + b_ref[...]


def _ada(c_rows, w, b):
    rows, d = c_rows.shape
    n = w.shape[1]
    tn = _pick_tile(n, ADA_COLS)
    return pl.pallas_call(
        _ada_kernel,
        out_shape=jax.ShapeDtypeStruct((rows, n), F32),
        grid=(n // tn,),
        in_specs=[pl.BlockSpec((rows, d), lambda j: (0, 0)),
                  pl.BlockSpec((d, tn), lambda j: (0, j)),
                  pl.BlockSpec((1, tn), lambda j: (0, j))],
        out_specs=pl.BlockSpec((rows, tn), lambda j: (0, j)),
        compiler_params=_params(1),
        name="ada",
    )(c_rows, w, b)


def _ffn_kernel(x_ref, m_ref, g_ref, wi_ref, wo_ref, *rest, k, final):
    x = x_ref[...]
    m = m_ref[0]
    if final:
        yf_ref, yb_ref, z_ref, a_ref, sn_ref, ws_ref, wm_ref, fg_ref, o_ref, acc_ref = rest
        y = (yf_ref[...] + yb_ref[...]) * _silu(z_ref[...])
        ssd = _rms(y, sn_ref[...]).astype(BF16)
        x = x + m[k - 1:k] * (_dot(ssd, ws_ref[...]) + _dot(a_ref[...], wm_ref[...]))
    else:
        o_ref, acc_ref = rest
    shift, scale, gate = m[k:k + 1], m[k + 1:k + 2], m[k + 2:k + 3]
    hn = (_rms(x, g_ref[...]) * (1 + scale) + shift).astype(BF16)
    for j in range(D_FF // FF_CHUNK):
        lo = j * FF_CHUNK
        g = _dot(hn, wi_ref[:, lo:lo + FF_CHUNK])
        u = _dot(hn, wi_ref[:, D_FF + lo:D_FF + lo + FF_CHUNK])
        a = (_silu(g) * u).astype(BF16)
        part = _dot(a, wo_ref[lo:lo + FF_CHUNK, :])
        if j == 0:
            acc_ref[...] = part
        else:
            acc_ref[...] += part
    y = x + 0.5 * gate * acc_ref[...]
    if final:
        y = _rms(y, fg_ref[...])
    o_ref[...] = y


def _ffn(x2d, m_all, m_index, g, wi, wo, k, tm, mix=None):
    n, d = x2d.shape
    final = mix is not None
    tok = lambda c: pl.BlockSpec((tm, c), lambda i: (i, 0))
    in_specs = [tok(d),
                pl.BlockSpec((1, N_MOD, d), lambda i: (m_index(i), 0, 0)),
                _const_spec((1, d)),
                _const_spec(wi.shape),
                _const_spec(wo.shape)]
    args = [x2d, m_all, g, wi, wo]
    if final:
        in_specs += [tok(a.shape[1]) for a in mix[:4]] + [_const_spec(a.shape) for a in mix[4:]]
        args += list(mix)
    return pl.pallas_call(
        functools.partial(_ffn_kernel, k=k, final=final),
        out_shape=jax.ShapeDtypeStruct((n, d), F32),
        grid=(n // tm,),
        in_specs=in_specs,
        out_specs=pl.BlockSpec((tm, d), lambda i: (i, 0)),
        scratch_shapes=[pltpu.VMEM((tm, d), F32)],
        compiler_params=_params(1),
        name="ffn_final" if final else "ffn",
    )(*args)


def _in_proj_kernel(x_ref, xp_ref, xn_ref, m_ref, g_ref, w_ref, cw_ref, cb_ref, kvn_ref, wuk_ref,
                    wuv_ref, tk_ref, *rest, with_q, n_alias):
    if with_q:
        qn_ref, wqa_ref, tqc_ref, tqs_ref = rest[:4]
        rest = rest[4:]
    rest = rest[n_alias:]
    u_ref, small_ref, k_ref, v_ref = rest[:4]
    if with_q:
        z_ref, q_ref = rest[4:6]
    ext_ref = rest[-1]
    m = m_ref[0]
    tm = x_ref.shape[1]
    i = pl.program_id(1)
    x_ext = jnp.concatenate([x_ref[0], xp_ref[0], xn_ref[0]], axis=0)
    hn_ext = (_rms(x_ext, g_ref[...]) * (1 + m[4:5]) + m[3:4]).astype(BF16)
    hn = hn_ext[0:tm]
    o_xbc = SSD_WIDTH
    o_cq = o_xbc + CONV_CH
    o_ckv = o_cq + Q_LORA
    o_small = o_ckv + KV_LORA

    xbc = _dot(hn_ext, w_ref[:, o_xbc:o_cq])
    if with_q:
        z_ref[0] = _dot(hn, w_ref[:, 0:o_xbc])
        cq_raw = _dot(hn, w_ref[:, o_cq:o_ckv])
    ckv_raw = _dot(hn, w_ref[:, o_ckv:o_small])
    small = _dot(hn, w_ref[:, o_small:o_small + SMALL_W])
    small_ref[0] = small

    has_prev = (i > 0).astype(F32)
    has_next = (i < pl.num_programs(1) - 1).astype(F32)
    ext_ref[0:SUBLANES] = xbc[tm:tm + SUBLANES] * has_prev
    ext_ref[SUBLANES:SUBLANES + tm] = xbc[0:tm]
    ext_ref[SUBLANES + tm:] = xbc[tm + SUBLANES:] * has_next
    conv = cb_ref[...]
    for tap in range(SSD_CONV):
        conv = conv + ext_ref[pl.ds(SUBLANES - SSD_CONV // 2 + tap, tm), :] * cw_ref[tap:tap + 1, :]
    u_ref[0] = _silu(conv).astype(BF16)

    ckv = _rms(ckv_raw, kvn_ref[...]).astype(BF16)
    t = small * tk_ref[...]
    lane = lax.broadcasted_iota(jnp.int32, (tm, LANES), 1)
    on_rope = jnp.logical_and(lane >= KPE_LANE0, lane < KPE_LANE0 + QK_ROPE)
    kpe = jnp.where(on_rope, t + pltpu.roll(t, QK_ROPE, axis=1), 0.0)
    k_nope = _dot(ckv, wuk_ref[...])
    for h in range(MLA_HEADS):
        hs = slice(h * HEAD_PAD, (h + 1) * HEAD_PAD)
        k_ref[0, :, hs] = (k_nope[:, hs] + kpe).astype(BF16)
    vt = _dot(ckv, wuv_ref[...]).T.astype(BF16)
    for n in range(tm // KEY_BLOCK):
        v_ref[0, n] = vt[:, n * KEY_BLOCK:(n + 1) * KEY_BLOCK]
    if with_q:
        cq = _rms(cq_raw, qn_ref[...]).astype(BF16)
        qa = _dot(cq, wqa_ref[...])
        tqc, tqs = tqc_ref[...], tqs_ref[...]
        first_half = lax.broadcasted_iota(jnp.int32, (tm, HEAD_PAD), 1) % (QK_ROPE // 2) < QK_ROPE // 4
        for h in range(MLA_HEADS):
            hs = slice(h * HEAD_PAD, (h + 1) * HEAD_PAD)
            qh = qa[:, hs]
            partner = jnp.where(first_half, pltpu.roll(qh, HEAD_PAD - QK_ROPE // 4, axis=1),
                                pltpu.roll(qh, QK_ROPE // 4, axis=1))
            q_ref[0, :, hs] = (qh * tqc + partner * tqs).astype(BF16)


def _in_proj(h3d, m_all, m_index, g, w_all, cw, cb, kvn, wuk, wuv, tk, tm, row0, s_total,
             q_args=None, alias_bufs=()):
    b, rows, d = h3d.shape
    assert tm % KEY_BLOCK == 0 and row0 % tm == 0
    with_q = q_args is not None
    blk0 = row0 // tm
    hb, n8 = tm // SUBLANES, rows // SUBLANES
    tok = lambda c: pl.BlockSpec((1, tm, c), lambda bi, i: (bi, i, 0))
    comb = lambda c: pl.BlockSpec((1, tm, c), lambda bi, i: (bi, blk0 + i, 0))
    table = pl.BlockSpec((tm, LANES), lambda bi, i: (i, 0))
    in_specs = [tok(d),
                pl.BlockSpec((1, SUBLANES, d), lambda bi, i: (bi, jnp.maximum(i * hb - 1, 0), 0)),
                pl.BlockSpec((1, SUBLANES, d), lambda bi, i: (bi, jnp.minimum(i * hb + hb, n8 - 1), 0)),
                pl.BlockSpec((1, N_MOD, d), lambda bi, i: (m_index(bi), 0, 0)),
                _const_spec(g.shape), _const_spec(w_all.shape), _const_spec(cw.shape), _const_spec(cb.shape),
                _const_spec(kvn.shape), _const_spec(wuk.shape), _const_spec(wuv.shape), table]
    args = [h3d, h3d, h3d, m_all, g, w_all, cw, cb, kvn, wuk, wuv, tk]
    out_shape = [jax.ShapeDtypeStruct((b, s_total, CONV_CH), BF16),
                 jax.ShapeDtypeStruct((b, s_total, SMALL_W), F32),
                 jax.ShapeDtypeStruct((b, s_total, MLA_PAD), BF16),
                 jax.ShapeDtypeStruct((b, s_total // KEY_BLOCK, MLA_HEADS * V_DIM, KEY_BLOCK), BF16)]
    out_specs = [comb(CONV_CH), comb(SMALL_W), comb(MLA_PAD),
                 pl.BlockSpec((1, tm // KEY_BLOCK, MLA_HEADS * V_DIM, KEY_BLOCK),
                              lambda bi, i: (bi, blk0 + i, 0, 0))]
    if with_q:
        qn, wqa, tqc, tqs = q_args
        in_specs += [_const_spec(qn.shape), _const_spec(wqa.shape), table, table]
        args += [qn, wqa, tqc, tqs]
        out_shape += [jax.ShapeDtypeStruct((b, rows, SSD_WIDTH), F32),
                      jax.ShapeDtypeStruct((b, rows, MLA_PAD), BF16)]
        out_specs += [tok(SSD_WIDTH), tok(MLA_PAD)]
    aliases = {len(args) + j: j for j in range(len(alias_bufs))}
    in_specs += [pl.BlockSpec(memory_space=pl.ANY)] * len(alias_bufs)
    args += list(alias_bufs)
    return pl.pallas_call(
        functools.partial(_in_proj_kernel, with_q=with_q, n_alias=len(alias_bufs)),
        out_shape=out_shape,
        grid=(b, rows // tm),
        in_specs=in_specs,
        out_specs=out_specs,
        input_output_aliases=aliases,
        scratch_shapes=[pltpu.VMEM((tm + 2 * SUBLANES, CONV_CH), F32)],
        compiler_params=_params(2),
        name="in_proj_q" if with_q else "in_proj_ctx",
    )(*args)


def _ssd_kernel(uf_ref, sf_ref, ub_ref, sb_ref, dtb_ref, aneg_ref, dskip_ref, ex_ref, yf_ref, yb_ref,
                hf_ref, hb_ref, *, ncb):
    q = SSD_CHUNK
    t = pl.program_id(1)
    n_sub = uf_ref.shape[1] // q
    u_refs, small_refs, h_refs, y_refs = (uf_ref, ub_ref), (sf_ref, sb_ref), (hf_ref, hb_ref), (yf_ref, yb_ref)
    lanes = [(d, k if d == 0 else n_sub - 1 - k) for k in range(n_sub) for d in (0, 1)]
    rows = [slice(sub * q, (sub + 1) * q) for _, sub in lanes]
    ids = range(len(lanes))

    @pl.when(t == 0)
    def _():
        hf_ref[...] = jnp.zeros_like(hf_ref)
        hb_ref[...] = jnp.zeros_like(hb_ref)

    o_b = SSD_WIDTH
    o_c = SSD_WIDTH + SSD_GROUPS * SSD_STATE
    xs = [u_refs[d][0, rows[i], 0:o_b].astype(F32) for i, (d, _) in enumerate(lanes)]
    bm = [u_refs[d][0, rows[i], o_b:o_c] for i, (d, _) in enumerate(lanes)]
    cm = [u_refs[d][0, rows[i], o_c:CONV_CH] for i, (d, _) in enumerate(lanes)]
    gslice = [slice(g * SSD_STATE, (g + 1) * SSD_STATE) for g in range(SSD_GROUPS)]
    cbm = [[_dot_nt(cm[i][:, gs], bm[i][:, gs]) for i in ids] for gs in gslice]

    ri = lax.broadcasted_iota(jnp.int32, (q, q), 0)
    ci = lax.broadcasted_iota(jnp.int32, (q, q), 1)
    keep = [ri >= ci, ri <= ci]
    last = [q - 1, 0]
    tri = [jnp.concatenate([jnp.where(keep[d], 1.0, 0.0).astype(BF16)] * 3, axis=1) for d in (0, 1)]
    dtv, parts = [], []
    for i, (d, _) in enumerate(lanes):
        pre = small_refs[d][0, rows[i], :] + dtb_ref[...]
        dtv.append(jnp.maximum(pre, 0.0) + jnp.log(1.0 + jnp.exp(-jnp.abs(pre))))
        a = dtv[i] * aneg_ref[...]
        a1 = a.astype(BF16)
        r1 = a - a1.astype(F32)
        a2 = r1.astype(BF16)
        a3 = (r1 - a2.astype(F32)).astype(BF16)
        parts.append(jnp.concatenate([a1, a2, a3], axis=0))
    acum = [_dot(tri[lanes[i][0]], parts[i]) for i in ids]
    acum_t = [acum[i].T for i in ids]
    dend = [jnp.exp(acum[i][last[d]:last[d] + 1, :] - acum[i]) for i, (d, _) in enumerate(lanes)]
    dstart = [jnp.exp(acum[i]) for i in ids]

    head_lane = lambda d, h: DT_LANE0 + SSD_HEADS * d + h

    def expand(arr, d):
        hi = arr.astype(BF16)
        lo = (arr - hi.astype(F32)).astype(BF16)
        return _dot(jnp.concatenate([hi, lo], axis=1), ex_ref[d])

    dstart_x = [expand(dstart[i], d) for i, (d, _) in enumerate(lanes)]
    xdt = [xs[i] * expand(dtv[i], d) for i, (d, _) in enumerate(lanes)]
    xend = [(xdt[i] * expand(dend[i], d)).astype(BF16) for i, (d, _) in enumerate(lanes)]
    gw = SSD_WIDTH // SSD_GROUPS
    hpg = SSD_HEADS // SSD_GROUPS
    glane = lax.broadcasted_iota(jnp.int32, (q, gw), 1)
    ys = [[] for _ in ids]
    for g in range(SSD_GROUPS):
        gs = gslice[g]
        gc = slice(g * gw, (g + 1) * gw)
        yg = [None for _ in ids]
        for h0 in range(0, hpg, 2):
            for i, (d, _) in enumerate(lanes):
                lmats, xmasks = [], []
                for hh in (h0, h0 + 1):
                    hl = head_lane(d, g * hpg + hh)
                    seg = acum[i][:, hl:hl + 1] - acum_t[i][hl:hl + 1, :]
                    lmats.append((cbm[g][i] * jnp.exp(jnp.where(keep[d], seg, -jnp.inf))).astype(BF16))
                    in_head = jnp.logical_and(glane >= hh * SSD_HEAD_DIM, glane < (hh + 1) * SSD_HEAD_DIM)
                    xmasks.append(jnp.where(in_head, xdt[i][:, gc], 0.0).astype(BF16))
                part = _dot(jnp.concatenate(lmats, axis=1), jnp.concatenate(xmasks, axis=0))
                yg[i] = part if yg[i] is None else yg[i] + part
        for i, (d, _) in enumerate(lanes):
            hst = h_refs[d][g]
            yoff = _dot(cm[i][:, gs], hst.astype(BF16)) * dstart_x[i][:, gc]
            ys[i].append(yg[i] + yoff)
            h_refs[d][g] = (dstart_x[i][last[d]:last[d] + 1, gc] * hst
                            + _dot_tn(bm[i][:, gs], xend[i][:, gc]))

    @pl.when(t >= ncb)
    def _():
        for i, (d, _) in enumerate(lanes):
            y = jnp.concatenate(ys[i], axis=1)
            if d == 0:
                y = y + dskip_ref[...] * xs[i]
            y_refs[d][0, rows[i], :] = y


def _ssd(u_all, small_all, dtb, aneg, dskip, l_x, l_c):
    b = u_all.shape[0]
    n_sub = SSD_BLOCK_CHUNKS if (l_x % (SSD_BLOCK_CHUNKS * SSD_CHUNK) == 0
                                 and l_c % (SSD_BLOCK_CHUNKS * SSD_CHUNK) == 0) else 1
    rb = n_sub * SSD_CHUNK
    nxb, ncb = l_x // rb, l_c // rb
    fwd = lambda t: jnp.where(t < ncb, nxb + t, t - ncb)
    bwd = lambda t: nxb + ncb - 1 - t
    block_spec = lambda w, order: pl.BlockSpec((1, rb, w), lambda bi, t: (bi, order(t), 0))
    y_spec = lambda order: pl.BlockSpec((1, rb, SSD_WIDTH), lambda bi, t: (bi, order(jnp.maximum(t, ncb)), 0))
    y_shape = jax.ShapeDtypeStruct((b, l_x, SSD_WIDTH), F32)
    h_buf = pltpu.VMEM((SSD_GROUPS, SSD_STATE, SSD_WIDTH // SSD_GROUPS), F32)
    ex_np = np.zeros((N_DIR, SMALL_W, SSD_WIDTH), np.float32)
    for dd in range(N_DIR):
        for h in range(SSD_HEADS):
            ex_np[dd, DT_LANE0 + SSD_HEADS * dd + h, h * SSD_HEAD_DIM:(h + 1) * SSD_HEAD_DIM] = 1.0
    ex = jnp.asarray(np.concatenate([ex_np, ex_np], axis=1), BF16)
    return pl.pallas_call(
        functools.partial(_ssd_kernel, ncb=ncb),
        out_shape=[y_shape, y_shape],
        grid=(b, nxb + ncb),
        in_specs=[block_spec(CONV_CH, fwd), block_spec(SMALL_W, fwd),
                  block_spec(CONV_CH, bwd), block_spec(SMALL_W, bwd),
                  _const_spec(dtb.shape), _const_spec(aneg.shape), _const_spec(dskip.shape),
                  _const_spec(ex.shape)],
        out_specs=[y_spec(fwd), y_spec(bwd)],
        scratch_shapes=[h_buf, h_buf],
        compiler_params=_params(2),
        name="ssd",
    )(u_all, small_all, u_all, small_all, dtb, aneg, dskip, ex)


def _attn_kernel(q_ref, k_ref, vt_ref, o_ref):
    nb = vt_ref.shape[1]
    kb = vt_ref.shape[3]
    heads = range(ATTN_HEADS)
    ones = jnp.ones((2 * SUBLANES, kb), BF16)
    q = [q_ref[0, :, hh * HEAD_PAD:(hh + 1) * HEAD_PAD] for hh in heads]

    def scores(j):
        return [_dot_nt(k_ref[0, j * kb:(j + 1) * kb, hh * HEAD_PAD:(hh + 1) * HEAD_PAD], q[hh])
                for hh in heads]

    m = [None] * ATTN_HEADS
    acc = [None] * ATTN_HEADS
    pending = [scores(j) for j in range(min(ATTN_LOOKAHEAD, nb))]
    for j in range(nb):
        s = pending.pop(0)
        if j + ATTN_LOOKAHEAD < nb:
            pending.append(scores(j + ATTN_LOOKAHEAD))
        for hh in heads:
            vt = jnp.concatenate([vt_ref[0, j, hh * V_DIM:(hh + 1) * V_DIM, :], ones], axis=0)
            m_blk = jnp.max(s[hh], axis=0, keepdims=True)
            m_new = m_blk if j == 0 else jnp.maximum(m[hh], m_blk)
            part = _dot(vt, jnp.exp2(s[hh] - m_new).astype(BF16))
            acc[hh] = part if j == 0 else acc[hh] * jnp.exp2(m[hh] - m_new) + part
            m[hh] = m_new

    outs = [(acc[hh][0:V_DIM] / acc[hh][V_DIM:V_DIM + 1]).T for hh in heads]
    o_ref[0] = jnp.concatenate(outs, axis=1).astype(BF16)


def _attn(q, k_all, vt_all, tq):
    b, l_x, _ = q.shape
    s_total = k_all.shape[1]
    nb, kb = vt_all.shape[1], vt_all.shape[3]
    hw = ATTN_HEADS * HEAD_PAD
    return pl.pallas_call(
        _attn_kernel,
        out_shape=jax.ShapeDtypeStruct((b, l_x, MLA_HEADS * V_DIM), BF16),
        grid=(b, MLA_HEADS // ATTN_HEADS, l_x // tq),
        in_specs=[pl.BlockSpec((1, tq, hw), lambda bi, h, i: (bi, i, h)),
                  pl.BlockSpec((1, s_total, hw), lambda bi, h, i: (bi, 0, h)),
                  pl.BlockSpec((1, nb, ATTN_HEADS * V_DIM, kb), lambda bi, h, i: (bi, 0, h, 0))],
        out_specs=pl.BlockSpec((1, tq, ATTN_HEADS * V_DIM), lambda bi, h, i: (bi, i, h)),
        compiler_params=_params(3),
        name="attn",
    )(q, k_all, vt_all)


def _rope_swap(w):
    parts = []
    for base in (0, QK_ROPE // 2):
        u1 = w[..., base:base + QK_ROPE // 4]
        u2 = w[..., base + QK_ROPE // 4:base + QK_ROPE // 2]
        parts += [-u2, u1]
    return jnp.concatenate(parts, axis=-1)


def _rope_tables(l_x):
    half = QK_ROPE // 2
    inv = (ROPE_BASE ** (-np.arange(0, half, 2, dtype=np.float32) / half)).astype(np.float32)
    rows = l_x // GRID_W
    row = np.repeat(np.arange(rows), GRID_W).astype(np.float32)
    col = (np.arange(rows * GRID_W) % GRID_W).astype(np.float32)
    ar = (row[:, None] * inv).astype(np.float64)
    ac = (col[:, None] * inv).astype(np.float64)
    cos = np.concatenate([np.cos(ar), np.cos(ar), np.cos(ac), np.cos(ac)], axis=1).astype(np.float32)
    sin = np.concatenate([np.sin(ar), np.sin(ar), np.sin(ac), np.sin(ac)], axis=1).astype(np.float32)
    return cos, sin


def kernel(x, c, ctx, c_ctx, w_ada, b_ada, norm_ffn1, w_ffn1_in, w_ffn1_out, norm_mix, w_in, conv_w, conv_b, dt_bias, a_log, d_skip, ssd_norm, q_norm, w_uq, kv_norm, w_ukv, w_out, norm_ffn2, w_ffn2_in, w_ffn2_out, final_norm):
    b, l_x, d = x.shape
    l_c = ctx.shape[1]
    assert d == D_MODEL and w_ada.shape[0] == 1
    assert l_x % KEY_BLOCK == 0 and l_c % KEY_BLOCK == 0 and l_x % GRID_W == 0
    s_total = l_x + l_c

    c_rows = jnp.concatenate([c, c_ctx[None, :]], axis=0)
    pad = (-c_rows.shape[0]) % SUBLANES
    c_rows = jnp.pad(c_rows, ((0, pad), (0, 0)))
    m_all = _ada(c_rows, w_ada[0], b_ada)[:b + 1].reshape(b + 1, N_MOD, d)

    tm = _pick_tile(l_x, FFN_TILE)
    tmc = _pick_tile(b * l_c, FFN_TILE)
    x_tiles = l_x // tm
    m_lat = lambda i: i // x_tiles
    m_ctx = lambda i: b

    wi1, wo1 = w_ffn1_in[0].astype(BF16), w_ffn1_out[0].astype(BF16)
    tm1 = _pick_tile(l_x, FFN1_TILE)
    x1 = _ffn(x.reshape(b * l_x, d), m_all, lambda i: i // (l_x // tm1), norm_ffn1, wi1, wo1, 0, tm1)
    c1 = _ffn(ctx.reshape(b * l_c, d), m_all, m_ctx, norm_ffn1, wi1, wo1, 0, tmc)

    w = w_in[0].astype(BF16)
    w_kpe = w[:, SPLIT_CKV:]
    n_dt = N_DIR * SSD_HEADS
    small = jnp.concatenate([w[:, SPLIT_XBC:SPLIT_DT], jnp.zeros((d, KPE_LANE0 - QK_ROPE - n_dt), BF16),
                             _rope_swap(w_kpe), w_kpe,
                             jnp.zeros((d, SMALL_W - KPE_LANE0 - QK_ROPE), BF16)], axis=1)
    w_all = jnp.concatenate([w[:, :SPLIT_XBC], w[:, SPLIT_DT:SPLIT_CKV], small], axis=1)
    hq = QK_NOPE + QK_ROPE
    wq = w_uq[0].reshape(Q_LORA, MLA_HEADS, hq)
    zq = jnp.zeros((Q_LORA, MLA_HEADS, HEAD_PAD - hq), F32)
    wqa = jnp.concatenate([wq, zq], axis=2).reshape(Q_LORA, MLA_PAD).astype(BF16)
    wkv = w_ukv[0].reshape(KV_LORA, MLA_HEADS, QK_NOPE + V_DIM)
    wuk = jnp.concatenate([wkv[:, :, :QK_NOPE], jnp.zeros((KV_LORA, MLA_HEADS, HEAD_PAD - QK_NOPE), F32)],
                          axis=2).reshape(KV_LORA, MLA_PAD).astype(BF16)
    wuv = wkv[:, :, QK_NOPE:].reshape(KV_LORA, MLA_HEADS * V_DIM).astype(BF16)
    cos, sin = _rope_tables(l_x)
    zpad = lambda n: np.zeros((l_x, n), np.float32)
    q_scale = np.float32(MLA_SCALE * math.log2(math.e))
    tqc = jnp.asarray(q_scale * np.concatenate([np.ones((l_x, QK_NOPE), np.float32), cos, zpad(HEAD_PAD - hq)], axis=1))
    sign = np.tile(np.repeat(np.float32([-1.0, 1.0]), QK_ROPE // 4), 2)
    tqs = jnp.asarray(q_scale * np.concatenate([zpad(QK_NOPE), sin * sign, zpad(HEAD_PAD - hq)], axis=1))
    tk_lat = jnp.asarray(np.concatenate([zpad(KPE_LANE0 - QK_ROPE), sin, cos,
                                         zpad(SMALL_W - KPE_LANE0 - QK_ROPE)], axis=1))
    tk_ctx_np = np.zeros((l_c, SMALL_W), np.float32)
    tk_ctx_np[:, KPE_LANE0:KPE_LANE0 + QK_ROPE] = 1.0
    tk_ctx = jnp.asarray(tk_ctx_np)

    cw = jnp.pad(conv_w[0], ((0, SUBLANES - SSD_CONV), (0, 0)))
    shared = (norm_mix, w_all, cw, conv_b, kv_norm, wuk, wuv)
    u_all, small_all, k_all, v_all, z, q = _in_proj(
        x1.reshape(b, l_x, d), m_all, lambda bi: bi, *shared, tk_lat, _pick_tile(l_x, PROJ_TILE), 0, s_total,
        q_args=(q_norm, wqa, tqc, tqs))
    u_all, small_all, k_all, v_all = _in_proj(
        c1.reshape(b, l_c, d), m_all, lambda bi: b, *shared, tk_ctx,
        _pick_tile(math.gcd(l_c, l_x), PROJ_CTX_TILE), l_x, s_total,
        alias_bufs=(u_all, small_all, k_all, v_all))

    lane_pad = lambda v: jnp.pad(v.reshape(1, -1), ((0, 0), (DT_LANE0, SMALL_W - DT_LANE0 - v.size)))
    dtb = lane_pad(dt_bias[0])
    aneg = lane_pad(-jnp.exp(a_log[0].astype(F32)))
    dskip = jnp.repeat(d_skip[0], SSD_HEAD_DIM)[None, :]
    yf, yb = _ssd(u_all, small_all, dtb, aneg, dskip, l_x, l_c)

    mla = _attn(q, k_all, v_all, _pick_tile(l_x, ATTN_Q_TILE))

    wo = w_out[0]
    ws = wo[:SSD_WIDTH].astype(BF16)
    wm = wo[SSD_WIDTH:].astype(BF16)
    n = b * l_x
    mix = (yf.reshape(n, SSD_WIDTH), yb.reshape(n, SSD_WIDTH), z.reshape(n, SSD_WIDTH),
           mla.reshape(n, MLA_HEADS * V_DIM), ssd_norm, ws, wm, final_norm[None, :])
    out = _ffn(x1, m_all, m_lat, norm_ffn2, w_ffn2_in[0].astype(BF16), w_ffn2_out[0].astype(BF16), 6, tm,
               mix=mix)
    return out.reshape(b, l_x, d)
```

```python
import functools
import math

import jax
import jax.numpy as jnp
import numpy as np
from jax import lax
from jax.experimental import pallas as pl
from jax.experimental.pallas import tpu as pltpu

F32 = jnp.float32
BF16 = jnp.bfloat16

D_MODEL = 1024
GRID_W = 64
N_MOD = 9
D_FF = 2816
EPS = 1e-6
SSD_HEADS = 8
SSD_HEAD_DIM = 64
SSD_WIDTH = SSD_HEADS * SSD_HEAD_DIM
SSD_GROUPS = 2
SSD_STATE = 128
SSD_CONV = 5
SSD_CHUNK = 128
N_DIR = 2
CONV_CH = SSD_WIDTH + 2 * SSD_GROUPS * SSD_STATE
MLA_HEADS = 8
QK_NOPE = 64
QK_ROPE = 32
V_DIM = 64
Q_LORA = 768
KV_LORA = 256
MLA_SCALE = (QK_NOPE + QK_ROPE) ** -0.5
ROPE_BASE = 10000.0
SPLIT_XBC = SSD_WIDTH + CONV_CH
SPLIT_DT = SPLIT_XBC + N_DIR * SSD_HEADS
SPLIT_CQ = SPLIT_DT + Q_LORA
SPLIT_CKV = SPLIT_CQ + KV_LORA

LANES = 128
SUBLANES = 8
VMEM_LIMIT = 56 * 1024 * 1024
ADA_VMEM_MIB = 16
SSD_VMEM_MIB = 16
ATTN_VMEM_MIB = 24

HEAD_PAD = LANES
MLA_PAD = MLA_HEADS * HEAD_PAD
SMALL_W = LANES
DT_LANE0 = 0
KPE_LANE0 = QK_NOPE
KEY_BLOCK = 256

ADA_COLS = 1024
FFN_TILE = 512
FFN1_TILE = 1024
FF_CHUNK = 256
PROJ_TILE = 1024
PROJ_CTX_TILE = 256
SSD_BLOCK_CHUNKS = 2
ATTN_Q_TILE = 1024
ATTN_HEADS = LANES // V_DIM
ATTN_LOOKAHEAD = 2


def _dot(a, b):
    return jnp.dot(a, b, preferred_element_type=F32)


def _dot_nt(a, b):
    return lax.dot_general(a, b, (((1,), (1,)), ((), ())), preferred_element_type=F32)


def _dot_tn(a, b):
    return lax.dot_general(a, b, (((0,), (0,)), ((), ())), preferred_element_type=F32)


def _silu(x):
    return x * (1.0 / (1.0 + jnp.exp(-x)))


def _rms(x, g):
    return x * lax.rsqrt(jnp.mean(x * x, axis=-1, keepdims=True) + EPS) * g


def _const_spec(shape):
    zeros = (0,) * len(shape)
    return pl.BlockSpec(shape, lambda *_: zeros, pipeline_mode=pl.Buffered(1))


def _params(n_axes, vmem_mib=None):
    limit = VMEM_LIMIT if vmem_mib is None else vmem_mib * 1024 * 1024
    return pltpu.CompilerParams(dimension_semantics=("arbitrary",) * n_axes, vmem_limit_bytes=limit)


def _pick_tile(n, pref):
    t = min(pref, n)
    while n % t:
        t //= 2
    return t


def _ada_kernel(c_ref, w_ref, b_ref, o_ref):
    s = _silu(c_ref[...]).astype(BF16)
    o_ref[...] = _dot(s, w_ref[...].astype(BF16)) + b_ref[...]


def _ada(c_rows, w, b):
    rows, d = c_rows.shape
    n = w.shape[1]
    tn = _pick_tile(n, ADA_COLS)
    return pl.pallas_call(
        _ada_kernel,
        out_shape=jax.ShapeDtypeStruct((rows, n), F32),
        grid=(n // tn,),
        in_specs=[pl.BlockSpec((rows, d), lambda j: (0, 0)),
                  pl.BlockSpec((d, tn), lambda j: (0, j)),
                  pl.BlockSpec((1, tn), lambda j: (0, j))],
        out_specs=pl.BlockSpec((rows, tn), lambda j: (0, j)),
        compiler_params=_params(1, ADA_VMEM_MIB),
        name="ada",
    )(c_rows, w, b)


def _ffn_kernel(x_ref, m_ref, g_ref, wi_ref, wo_ref, *rest, k, final):
    x = x_ref[...]
    m = m_ref[0]
    if final:
        yf_ref, yb_ref, z_ref, a_ref, sn_ref, ws_ref, wm_ref, fg_ref, o_ref, acc_ref = rest
        y = (yf_ref[...] + yb_ref[...]) * _silu(z_ref[...])
        ssd = _rms(y, sn_ref[...]).astype(BF16)
        x = x + m[k - 1:k] * (_dot(ssd, ws_ref[...]) + _dot(a_ref[...], wm_ref[...]))
    else:
        o_ref, acc_ref = rest
    shift, scale, gate = m[k:k + 1], m[k + 1:k + 2], m[k + 2:k + 3]
    hn = (_rms(x, g_ref[...]) * (1 + scale) + shift).astype(BF16)
    for j in range(D_FF // FF_CHUNK):
        lo = j * FF_CHUNK
        g = _dot(hn, wi_ref[:, lo:lo + FF_CHUNK])
        u = _dot(hn, wi_ref[:, D_FF + lo:D_FF + lo + FF_CHUNK])
        a = (_silu(g) * u).astype(BF16)
        part = _dot(a, wo_ref[lo:lo + FF_CHUNK, :])
        if j == 0:
            acc_ref[...] = part
        else:
            acc_ref[...] += part
    y = x + 0.5 * gate * acc_ref[...]
    if final:
        y = _rms(y, fg_ref[...])
    o_ref[...] = y


def _ffn(x2d, m_all, m_index, g, wi, wo, k, tm, mix=None):
    n, d = x2d.shape
    final = mix is not None
    tok = lambda c: pl.BlockSpec((tm, c), lambda i: (i, 0))
    in_specs = [tok(d),
                pl.BlockSpec((1, N_MOD, d), lambda i: (m_index(i), 0, 0)),
                _const_spec((1, d)),
                _const_spec(wi.shape),
                _const_spec(wo.shape)]
    args = [x2d, m_all, g, wi, wo]
    if final:
        in_specs += [tok(a.shape[1]) for a in mix[:4]] + [_const_spec(a.shape) for a in mix[4:]]
        args += list(mix)
    return pl.pallas_call(
        functools.partial(_ffn_kernel, k=k, final=final),
        out_shape=jax.ShapeDtypeStruct((n, d), F32),
        grid=(n // tm,),
        in_specs=in_specs,
        out_specs=pl.BlockSpec((tm, d), lambda i: (i, 0)),
        scratch_shapes=[pltpu.VMEM((tm, d), F32)],
        compiler_params=_params(1),
        name="ffn_final" if final else "ffn",
    )(*args)


def _in_proj_kernel(x_ref, xp_ref, xn_ref, m_ref, g_ref, w_ref, cw_ref, cb_ref, kvn_ref, wuk_ref,
                    wuv_ref, tk_ref, *rest, with_q, n_alias):
    if with_q:
        qn_ref, wqa_ref, tqc_ref, tqs_ref = rest[:4]
        rest = rest[4:]
    rest = rest[n_alias:]
    u_ref, small_ref, k_ref, v_ref = rest[:4]
    if with_q:
        z_ref, q_ref = rest[4:6]
    ext_ref = rest[-1]
    m = m_ref[0]
    tm = x_ref.shape[1]
    i = pl.program_id(1)
    x_ext = jnp.concatenate([x_ref[0], xp_ref[0], xn_ref[0]], axis=0)
    hn_ext = (_rms(x_ext, g_ref[...]) * (1 + m[4:5]) + m[3:4]).astype(BF16)
    hn = hn_ext[0:tm]
    o_xbc = SSD_WIDTH
    o_cq = o_xbc + CONV_CH
    o_ckv = o_cq + Q_LORA
    o_small = o_ckv + KV_LORA

    xbc = _dot(hn_ext, w_ref[:, o_xbc:o_cq])
    if with_q:
        z_ref[0] = _dot(hn, w_ref[:, 0:o_xbc])
        cq_raw = _dot(hn, w_ref[:, o_cq:o_ckv])
    ckv_raw = _dot(hn, w_ref[:, o_ckv:o_small])
    small = _dot(hn, w_ref[:, o_small:o_small + SMALL_W])
    small_ref[0] = small

    has_prev = (i > 0).astype(F32)
    has_next = (i < pl.num_programs(1) - 1).astype(F32)
    ext_ref[0:SUBLANES] = xbc[tm:tm + SUBLANES] * has_prev
    ext_ref[SUBLANES:SUBLANES + tm] = xbc[0:tm]
    ext_ref[SUBLANES + tm:] = xbc[tm + SUBLANES:] * has_next
    conv = cb_ref[...]
    for tap in range(SSD_CONV):
        conv = conv + ext_ref[pl.ds(SUBLANES - SSD_CONV // 2 + tap, tm), :] * cw_ref[tap:tap + 1, :]
    u_ref[0] = _silu(conv).astype(BF16)

    ckv = _rms(ckv_raw, kvn_ref[...]).astype(BF16)
    t = small * tk_ref[...]
    lane = lax.broadcasted_iota(jnp.int32, (tm, LANES), 1)
    on_rope = jnp.logical_and(lane >= KPE_LANE0, lane < KPE_LANE0 + QK_ROPE)
    kpe = jnp.where(on_rope, t + pltpu.roll(t, QK_ROPE, axis=1), 0.0)
    k_nope = _dot(ckv, wuk_ref[...])
    for h in range(MLA_HEADS):
        hs = slice(h * HEAD_PAD, (h + 1) * HEAD_PAD)
        k_ref[0, :, hs] = (k_nope[:, hs] + kpe).astype(BF16)
    vt = _dot(ckv, wuv_ref[...]).T.astype(BF16)
    for n in range(tm // KEY_BLOCK):
        v_ref[0, n] = vt[:, n * KEY_BLOCK:(n + 1) * KEY_BLOCK]
    if with_q:
        cq = _rms(cq_raw, qn_ref[...]).astype(BF16)
        qa = _dot(cq, wqa_ref[...])
        tqc, tqs = tqc_ref[...], tqs_ref[...]
        first_half = lax.broadcasted_iota(jnp.int32, (tm, HEAD_PAD), 1) % (QK_ROPE // 2) < QK_ROPE // 4
        for h in range(MLA_HEADS):
            hs = slice(h * HEAD_PAD, (h + 1) * HEAD_PAD)
            qh = qa[:, hs]
            partner = jnp.where(first_half, pltpu.roll(qh, HEAD_PAD - QK_ROPE // 4, axis=1),
                                pltpu.roll(qh, QK_ROPE // 4, axis=1))
            q_ref[0, :, hs] = (qh * tqc + partner * tqs).astype(BF16)


def _in_proj(h3d, m_all, m_index, g, w_all, cw, cb, kvn, wuk, wuv, tk, tm, row0, s_total,
             q_args=None, alias_bufs=()):
    b, rows, d = h3d.shape
    assert tm % KEY_BLOCK == 0 and row0 % tm == 0
    with_q = q_args is not None
    blk0 = row0 // tm
    hb, n8 = tm // SUBLANES, rows // SUBLANES
    tok = lambda c: pl.BlockSpec((1, tm, c), lambda bi, i: (bi, i, 0))
    comb = lambda c: pl.BlockSpec((1, tm, c), lambda bi, i: (bi, blk0 + i, 0))
    table = pl.BlockSpec((tm, LANES), lambda bi, i: (i, 0))
    in_specs = [tok(d),
                pl.BlockSpec((1, SUBLANES, d), lambda bi, i: (bi, jnp.maximum(i * hb - 1, 0), 0)),
                pl.BlockSpec((1, SUBLANES, d), lambda bi, i: (bi, jnp.minimum(i * hb + hb, n8 - 1), 0)),
                pl.BlockSpec((1, N_MOD, d), lambda bi, i: (m_index(bi), 0, 0)),
                _const_spec(g.shape), _const_spec(w_all.shape), _const_spec(cw.shape), _const_spec(cb.shape),
                _const_spec(kvn.shape), _const_spec(wuk.shape), _const_spec(wuv.shape), table]
    args = [h3d, h3d, h3d, m_all, g, w_all, cw, cb, kvn, wuk, wuv, tk]
    out_shape = [jax.ShapeDtypeStruct((b, s_total, CONV_CH), BF16),
                 jax.ShapeDtypeStruct((b, s_total, SMALL_W), F32),
                 jax.ShapeDtypeStruct((b, s_total, MLA_PAD), BF16),
                 jax.ShapeDtypeStruct((b, s_total // KEY_BLOCK, MLA_HEADS * V_DIM, KEY_BLOCK), BF16)]
    out_specs = [comb(CONV_CH), comb(SMALL_W), comb(MLA_PAD),
                 pl.BlockSpec((1, tm // KEY_BLOCK, MLA_HEADS * V_DIM, KEY_BLOCK),
                              lambda bi, i: (bi, blk0 + i, 0, 0))]
    if with_q:
        qn, wqa, tqc, tqs = q_args
        in_specs += [_const_spec(qn.shape), _const_spec(wqa.shape), table, table]
        args += [qn, wqa, tqc, tqs]
        out_shape += [jax.ShapeDtypeStruct((b, rows, SSD_WIDTH), F32),
                      jax.ShapeDtypeStruct((b, rows, MLA_PAD), BF16)]
        out_specs += [tok(SSD_WIDTH), tok(MLA_PAD)]
    aliases = {len(args) + j: j for j in range(len(alias_bufs))}
    in_specs += [pl.BlockSpec(memory_space=pl.ANY)] * len(alias_bufs)
    args += list(alias_bufs)
    return pl.pallas_call(
        functools.partial(_in_proj_kernel, with_q=with_q, n_alias=len(alias_bufs)),
        out_shape=out_shape,
        grid=(b, rows // tm),
        in_specs=in_specs,
        out_specs=out_specs,
        input_output_aliases=aliases,
        scratch_shapes=[pltpu.VMEM((tm + 2 * SUBLANES, CONV_CH), F32)],
        compiler_params=_params(2),
        name="in_proj_q" if with_q else "in_proj_ctx",
    )(*args)


def _ssd_kernel(uf_ref, sf_ref, ub_ref, sb_ref, dtb_ref, aneg_ref, dskip_ref, ex_ref, yf_ref, yb_ref,
                hf_ref, hb_ref, *, ncb):
    q = SSD_CHUNK
    t = pl.program_id(1)
    n_sub = uf_ref.shape[1] // q
    u_refs, small_refs, h_refs, y_refs = (uf_ref, ub_ref), (sf_ref, sb_ref), (hf_ref, hb_ref), (yf_ref, yb_ref)
    lanes = [(d, k if d == 0 else n_sub - 1 - k) for k in range(n_sub) for d in (0, 1)]
    rows = [slice(sub * q, (sub + 1) * q) for _, sub in lanes]
    ids = range(len(lanes))

    @pl.when(t == 0)
    def _():
        hf_ref[...] = jnp.zeros_like(hf_ref)
        hb_ref[...] = jnp.zeros_like(hb_ref)

    o_b = SSD_WIDTH
    o_c = SSD_WIDTH + SSD_GROUPS * SSD_STATE
    xs = [u_refs[d][0, rows[i], 0:o_b].astype(F32) for i, (d, _) in enumerate(lanes)]
    bm = [u_refs[d][0, rows[i], o_b:o_c] for i, (d, _) in enumerate(lanes)]
    cm = [u_refs[d][0, rows[i], o_c:CONV_CH] for i, (d, _) in enumerate(lanes)]
    gslice = [slice(g * SSD_STATE, (g + 1) * SSD_STATE) for g in range(SSD_GROUPS)]
    cbm = [[_dot_nt(cm[i][:, gs], bm[i][:, gs]) for i in ids] for gs in gslice]

    ri = lax.broadcasted_iota(jnp.int32, (q, q), 0)
    ci = lax.broadcasted_iota(jnp.int32, (q, q), 1)
    keep = [ri >= ci, ri <= ci]
    last = [q - 1, 0]
    tri = [jnp.concatenate([jnp.where(keep[d], 1.0, 0.0).astype(BF16)] * 3, axis=1) for d in (0, 1)]
    dtv, parts = [], []
    for i, (d, _) in enumerate(lanes):
        pre = small_refs[d][0, rows[i], :] + dtb_ref[...]
        dtv.append(jnp.maximum(pre, 0.0) + jnp.log(1.0 + jnp.exp(-jnp.abs(pre))))
        a = dtv[i] * aneg_ref[...]
        a1 = a.astype(BF16)
        r1 = a - a1.astype(F32)
        a2 = r1.astype(BF16)
        a3 = (r1 - a2.astype(F32)).astype(BF16)
        parts.append(jnp.concatenate([a1, a2, a3], axis=0))
    acum = [_dot(tri[lanes[i][0]], parts[i]) for i in ids]
    acum_t = [acum[i].T for i in ids]
    dend = [jnp.exp(acum[i][last[d]:last[d] + 1, :] - acum[i]) for i, (d, _) in enumerate(lanes)]
    dstart = [jnp.exp(acum[i]) for i in ids]

    head_lane = lambda d, h: DT_LANE0 + SSD_HEADS * d + h

    def expand(arr, d):
        hi = arr.astype(BF16)
        lo = (arr - hi.astype(F32)).astype(BF16)
        return _dot(jnp.concatenate([hi, lo], axis=1), ex_ref[d])

    dstart_x = [expand(dstart[i], d) for i, (d, _) in enumerate(lanes)]
    xdt = [xs[i] * expand(dtv[i], d) for i, (d, _) in enumerate(lanes)]
    xend = [(xdt[i] * expand(dend[i], d)).astype(BF16) for i, (d, _) in enumerate(lanes)]
    gw = SSD_WIDTH // SSD_GROUPS
    hpg = SSD_HEADS // SSD_GROUPS
    glane = lax.broadcasted_iota(jnp.int32, (q, gw), 1)
    ys = [[] for _ in ids]
    for g in range(SSD_GROUPS):
        gs = gslice[g]
        gc = slice(g * gw, (g + 1) * gw)
        yg = [None for _ in ids]
        for h0 in range(0, hpg, 2):
            for i, (d, _) in enumerate(lanes):
                lmats, xmasks = [], []
                for hh in (h0, h0 + 1):
                    hl = head_lane(d, g * hpg + hh)
                    seg = acum[i][:, hl:hl + 1] - acum_t[i][hl:hl + 1, :]
                    lmats.append((cbm[g][i] * jnp.exp(jnp.where(keep[d], seg, -jnp.inf))).astype(BF16))
                    in_head = jnp.logical_and(glane >= hh * SSD_HEAD_DIM, glane < (hh + 1) * SSD_HEAD_DIM)
                    xmasks.append(jnp.where(in_head, xdt[i][:, gc], 0.0).astype(BF16))
                part = _dot(jnp.concatenate(lmats, axis=1), jnp.concatenate(xmasks, axis=0))
                yg[i] = part if yg[i] is None else yg[i] + part
        for i, (d, _) in enumerate(lanes):
            hst = h_refs[d][g]
            yoff = _dot(cm[i][:, gs], hst.astype(BF16)) * dstart_x[i][:, gc]
            ys[i].append(yg[i] + yoff)
            h_refs[d][g] = (dstart_x[i][last[d]:last[d] + 1, gc] * hst
                            + _dot_tn(bm[i][:, gs], xend[i][:, gc]))

    @pl.when(t >= ncb)
    def _():
        for i, (d, _) in enumerate(lanes):
            y = jnp.concatenate(ys[i], axis=1)
            if d == 0:
                y = y + dskip_ref[...] * xs[i]
            y_refs[d][0, rows[i], :] = y


def _ssd(u_all, small_all, dtb, aneg, dskip, l_x, l_c):
    b = u_all.shape[0]
    n_sub = SSD_BLOCK_CHUNKS if (l_x % (SSD_BLOCK_CHUNKS * SSD_CHUNK) == 0
                                 and l_c % (SSD_BLOCK_CHUNKS * SSD_CHUNK) == 0) else 1
    rb = n_sub * SSD_CHUNK
    nxb, ncb = l_x // rb, l_c // rb
    fwd = lambda t: jnp.where(t < ncb, nxb + t, t - ncb)
    bwd = lambda t: nxb + ncb - 1 - t
    block_spec = lambda w, order: pl.BlockSpec((1, rb, w), lambda bi, t: (bi, order(t), 0))
    y_spec = lambda order: pl.BlockSpec((1, rb, SSD_WIDTH), lambda bi, t: (bi, order(jnp.maximum(t, ncb)), 0))
    y_shape = jax.ShapeDtypeStruct((b, l_x, SSD_WIDTH), F32)
    h_buf = pltpu.VMEM((SSD_GROUPS, SSD_STATE, SSD_WIDTH // SSD_GROUPS), F32)
    ex_np = np.zeros((N_DIR, SMALL_W, SSD_WIDTH), np.float32)
    for dd in range(N_DIR):
        for h in range(SSD_HEADS):
            ex_np[dd, DT_LANE0 + SSD_HEADS * dd + h, h * SSD_HEAD_DIM:(h + 1) * SSD_HEAD_DIM] = 1.0
    ex = jnp.asarray(np.concatenate([ex_np, ex_np], axis=1), BF16)
    return pl.pallas_call(
        functools.partial(_ssd_kernel, ncb=ncb),
        out_shape=[y_shape, y_shape],
        grid=(b, nxb + ncb),
        in_specs=[block_spec(CONV_CH, fwd), block_spec(SMALL_W, fwd),
                  block_spec(CONV_CH, bwd), block_spec(SMALL_W, bwd),
                  _const_spec(dtb.shape), _const_spec(aneg.shape), _const_spec(dskip.shape),
                  _const_spec(ex.shape)],
        out_specs=[y_spec(fwd), y_spec(bwd)],
        scratch_shapes=[h_buf, h_buf],
        compiler_params=_params(2, SSD_VMEM_MIB),
        name="ssd",
    )(u_all, small_all, u_all, small_all, dtb, aneg, dskip, ex)


def _attn_kernel(q_ref, k_ref, vt_ref, o_ref):
    nb = vt_ref.shape[1]
    kb = vt_ref.shape[3]
    heads = range(ATTN_HEADS)
    ones = jnp.ones((2 * SUBLANES, kb), BF16)
    q = [q_ref[0, :, hh * HEAD_PAD:(hh + 1) * HEAD_PAD] for hh in heads]

    def scores(j):
        return [_dot_nt(k_ref[0, j * kb:(j + 1) * kb, hh * HEAD_PAD:(hh + 1) * HEAD_PAD], q[hh])
                for hh in heads]

    m = [None] * ATTN_HEADS
    acc = [None] * ATTN_HEADS
    pending = [scores(j) for j in range(min(ATTN_LOOKAHEAD, nb))]
    for j in range(nb):
        s = pending.pop(0)
        if j + ATTN_LOOKAHEAD < nb:
            pending.append(scores(j + ATTN_LOOKAHEAD))
        for hh in heads:
            vt = jnp.concatenate([vt_ref[0, j, hh * V_DIM:(hh + 1) * V_DIM, :], ones], axis=0)
            m_blk = jnp.max(s[hh], axis=0, keepdims=True)
            m_new = m_blk if j == 0 else jnp.maximum(m[hh], m_blk)
            part = _dot(vt, jnp.exp2(s[hh] - m_new).astype(BF16))
            acc[hh] = part if j == 0 else acc[hh] * jnp.exp2(m[hh] - m_new) + part
            m[hh] = m_new

    outs = [(acc[hh][0:V_DIM] / acc[hh][V_DIM:V_DIM + 1]).T for hh in heads]
    o_ref[0] = jnp.concatenate(outs, axis=1).astype(BF16)


def _attn(q, k_all, vt_all, tq):
    b, l_x, _ = q.shape
    s_total = k_all.shape[1]
    nb, kb = vt_all.shape[1], vt_all.shape[3]
    hw = ATTN_HEADS * HEAD_PAD
    return pl.pallas_call(
        _attn_kernel,
        out_shape=jax.ShapeDtypeStruct((b, l_x, MLA_HEADS * V_DIM), BF16),
        grid=(b, MLA_HEADS // ATTN_HEADS, l_x // tq),
        in_specs=[pl.BlockSpec((1, tq, hw), lambda bi, h, i: (bi, i, h)),
                  pl.BlockSpec((1, s_total, hw), lambda bi, h, i: (bi, 0, h)),
                  pl.BlockSpec((1, nb, ATTN_HEADS * V_DIM, kb), lambda bi, h, i: (bi, 0, h, 0))],
        out_specs=pl.BlockSpec((1, tq, ATTN_HEADS * V_DIM), lambda bi, h, i: (bi, i, h)),
        compiler_params=_params(3, ATTN_VMEM_MIB),
        name="attn",
    )(q, k_all, vt_all)


def _rope_swap(w):
    parts = []
    for base in (0, QK_ROPE // 2):
        u1 = w[..., base:base + QK_ROPE // 4]
        u2 = w[..., base + QK_ROPE // 4:base + QK_ROPE // 2]
        parts += [-u2, u1]
    return jnp.concatenate(parts, axis=-1)


def _rope_tables(l_x):
    half = QK_ROPE // 2
    inv = (ROPE_BASE ** (-np.arange(0, half, 2, dtype=np.float32) / half)).astype(np.float32)
    rows = l_x // GRID_W
    row = np.repeat(np.arange(rows), GRID_W).astype(np.float32)
    col = (np.arange(rows * GRID_W) % GRID_W).astype(np.float32)
    ar = (row[:, None] * inv).astype(np.float64)
    ac = (col[:, None] * inv).astype(np.float64)
    cos = np.concatenate([np.cos(ar), np.cos(ar), np.cos(ac), np.cos(ac)], axis=1).astype(np.float32)
    sin = np.concatenate([np.sin(ar), np.sin(ar), np.sin(ac), np.sin(ac)], axis=1).astype(np.float32)
    return cos, sin


def kernel(x, c, ctx, c_ctx, w_ada, b_ada, norm_ffn1, w_ffn1_in, w_ffn1_out, norm_mix, w_in, conv_w, conv_b, dt_bias, a_log, d_skip, ssd_norm, q_norm, w_uq, kv_norm, w_ukv, w_out, norm_ffn2, w_ffn2_in, w_ffn2_out, final_norm):
    b, l_x, d = x.shape
    l_c = ctx.shape[1]
    assert d == D_MODEL and w_ada.shape[0] == 1
    assert l_x % KEY_BLOCK == 0 and l_c % KEY_BLOCK == 0 and l_x % GRID_W == 0
    s_total = l_x + l_c

    c_rows = jnp.concatenate([c, c_ctx[None, :]], axis=0)
    pad = (-c_rows.shape[0]) % SUBLANES
    c_rows = jnp.pad(c_rows, ((0, pad), (0, 0)))
    m_all = _ada(c_rows, w_ada[0], b_ada)[:b + 1].reshape(b + 1, N_MOD, d)

    tm = _pick_tile(l_x, FFN_TILE)
    tmc = _pick_tile(b * l_c, FFN_TILE)
    x_tiles = l_x // tm
    m_lat = lambda i: i // x_tiles
    m_ctx = lambda i: b

    wi1, wo1 = w_ffn1_in[0].astype(BF16), w_ffn1_out[0].astype(BF16)
    tm1 = _pick_tile(l_x, FFN1_TILE)
    x1 = _ffn(x.reshape(b * l_x, d), m_all, lambda i: i // (l_x // tm1), norm_ffn1, wi1, wo1, 0, tm1)
    c1 = _ffn(ctx.reshape(b * l_c, d), m_all, m_ctx, norm_ffn1, wi1, wo1, 0, tmc)

    w = w_in[0].astype(BF16)
    w_kpe = w[:, SPLIT_CKV:]
    n_dt = N_DIR * SSD_HEADS
    small = jnp.concatenate([w[:, SPLIT_XBC:SPLIT_DT], jnp.zeros((d, KPE_LANE0 - QK_ROPE - n_dt), BF16),
                             _rope_swap(w_kpe), w_kpe,
                             jnp.zeros((d, SMALL_W - KPE_LANE0 - QK_ROPE), BF16)], axis=1)
    w_all = jnp.concatenate([w[:, :SPLIT_XBC], w[:, SPLIT_DT:SPLIT_CKV], small], axis=1)
    hq = QK_NOPE + QK_ROPE
    wq = w_uq[0].reshape(Q_LORA, MLA_HEADS, hq)
    zq = jnp.zeros((Q_LORA, MLA_HEADS, HEAD_PAD - hq), F32)
    wqa = jnp.concatenate([wq, zq], axis=2).reshape(Q_LORA, MLA_PAD).astype(BF16)
    wkv = w_ukv[0].reshape(KV_LORA, MLA_HEADS, QK_NOPE + V_DIM)
    wuk = jnp.concatenate([wkv[:, :, :QK_NOPE], jnp.zeros((KV_LORA, MLA_HEADS, HEAD_PAD - QK_NOPE), F32)],
                          axis=2).reshape(KV_LORA, MLA_PAD).astype(BF16)
    wuv = wkv[:, :, QK_NOPE:].reshape(KV_LORA, MLA_HEADS * V_DIM).astype(BF16)
    cos, sin = _rope_tables(l_x)
    zpad = lambda n: np.zeros((l_x, n), np.float32)
    q_scale = np.float32(MLA_SCALE * math.log2(math.e))
    tqc = jnp.asarray(q_scale * np.concatenate([np.ones((l_x, QK_NOPE), np.float32), cos, zpad(HEAD_PAD - hq)], axis=1))
    sign = np.tile(np.repeat(np.float32([-1.0, 1.0]), QK_ROPE // 4), 2)
    tqs = jnp.asarray(q_scale * np.concatenate([zpad(QK_NOPE), sin * sign, zpad(HEAD_PAD - hq)], axis=1))
    tk_lat = jnp.asarray(np.concatenate([zpad(KPE_LANE0 - QK_ROPE), sin, cos,
                                         zpad(SMALL_W - KPE_LANE0 - QK_ROPE)], axis=1))
    tk_ctx_np = np.zeros((l_c, SMALL_W), np.float32)
    tk_ctx_np[:, KPE_LANE0:KPE_LANE0 + QK_ROPE] = 1.0
    tk_ctx = jnp.asarray(tk_ctx_np)

    cw = jnp.pad(conv_w[0], ((0, SUBLANES - SSD_CONV), (0, 0)))
    shared = (norm_mix, w_all, cw, conv_b, kv_norm, wuk, wuv)
    u_all, small_all, k_all, v_all, z, q = _in_proj(
        x1.reshape(b, l_x, d), m_all, lambda bi: bi, *shared, tk_lat, _pick_tile(l_x, PROJ_TILE), 0, s_total,
        q_args=(q_norm, wqa, tqc, tqs))
    u_all, small_all, k_all, v_all = _in_proj(
        c1.reshape(b, l_c, d), m_all, lambda bi: b, *shared, tk_ctx,
        _pick_tile(math.gcd(l_c, l_x), PROJ_CTX_TILE), l_x, s_total,
        alias_bufs=(u_all, small_all, k_all, v_all))

    lane_pad = lambda v: jnp.pad(v.reshape(1, -1), ((0, 0), (DT_LANE0, SMALL_W - DT_LANE0 - v.size)))
    dtb = lane_pad(dt_bias[0])
    aneg = lane_pad(-jnp.exp(a_log[0].astype(F32)))
    dskip = jnp.repeat(d_skip[0], SSD_HEAD_DIM)[None, :]
    yf, yb = _ssd(u_all, small_all, dtb, aneg, dskip, l_x, l_c)

    mla = _attn(q, k_all, v_all, _pick_tile(l_x, ATTN_Q_TILE))

    wo = w_out[0]
    ws = wo[:SSD_WIDTH].astype(BF16)
    wm = wo[SSD_WIDTH:].astype(BF16)
    n = b * l_x
    mix = (yf.reshape(n, SSD_WIDTH), yb.reshape(n, SSD_WIDTH), z.reshape(n, SSD_WIDTH),
           mla.reshape(n, MLA_HEADS * V_DIM), ssd_norm, ws, wm, final_norm[None, :])
    out = _ffn(x1, m_all, m_lat, norm_ffn2, w_ffn2_in[0].astype(BF16), w_ffn2_out[0].astype(BF16), 6, tm,
               mix=mix)
    return out.reshape(b, l_x, d)
```
